```python
import math
import jax, jax.numpy as jnp
from jax import lax
import numpy as np

D_MODEL = 1024
BATCH = 16
SEQ = 2048
DEPTH = 1
DEC_BATCH = 128
DEC_SEQ = 8
PAST_LEN = 16384
PAGE_SIZE = 128

MLA_HEADS = 8
MLA_NOPE = 64
MLA_ROPE = 32
MLA_V = 64
Q_LORA = 256
KV_LORA = 128
MLA_ROW = KV_LORA + MLA_ROPE
MLA_SCALE = (MLA_NOPE + MLA_ROPE) ** -0.5
ROPE_THETA = 10000.0

DIFF_HEADS = 4
DIFF_DH = 64
DIFF_DV = 2 * DIFF_DH
DIFF_SCALE = DIFF_DH ** -0.5

N_BUCKETS = 32
MAX_DISTANCE = 128

N_EXPERTS = 64
N_GROUPS = 8
TOPK_GROUPS = 4
TOP_K = 8
D_EXPERT = 256
D_SHARED = 256
ROUTE_SCALE = 2.5

BRANCH_A = MLA_HEADS * MLA_V
BRANCH_B = DIFF_HEADS * DIFF_DV
D_IN = Q_LORA + KV_LORA + MLA_ROPE + 2 * DIFF_HEADS * 2 * DIFF_DH + DIFF_HEADS * DIFF_DV + 2 * D_MODEL

Q_BLOCK = 128
EPS = 1e-6
NEG_INF = -1e30

kernel_name = 'hybrid_mla_diffattn_moe_step'


def _rmsnorm(x, g):
    xf = x.astype(jnp.float32)
    y = xf * lax.rsqrt(jnp.mean(xf * xf, axis=-1, keepdims=True) + EPS)
    return (y * g.astype(jnp.float32)).astype(x.dtype)


def _rope(x, pos):
    half = x.shape[-1] // 2
    inv = ROPE_THETA ** (-jnp.arange(half, dtype=jnp.float32) / half)
    ang = pos.astype(jnp.float32)[:, None] * inv[None, :]
    cos = jnp.cos(ang)[None, :, None, :]
    sin = jnp.sin(ang)[None, :, None, :]
    xf = x.astype(jnp.float32)
    x1, x2 = xf[..., :half], xf[..., half:]
    return jnp.concatenate([x1 * cos - x2 * sin, x1 * sin + x2 * cos], axis=-1).astype(x.dtype)


def _t5_bias(table, q_pos, k_pos):
    n = jnp.maximum(q_pos[:, None] - k_pos[None, :], 0)
    max_exact = N_BUCKETS // 2
    nf = jnp.maximum(n, 1).astype(jnp.float32)
    large = max_exact + (jnp.log(nf / max_exact) / math.log(MAX_DISTANCE / max_exact)
                         * (N_BUCKETS - max_exact)).astype(jnp.int32)
    large = jnp.minimum(large, N_BUCKETS - 1)
    bucket = jnp.where(n < max_exact, n, large)
    return jnp.transpose(table[bucket], (2, 0, 1)).astype(jnp.float32)


def _mixer_inputs(h, pos, w_in, g_q, w_uq, g_kv, w_uk):
    B, S, _ = h.shape
    z = h @ w_in
    widths = (Q_LORA, KV_LORA, MLA_ROPE, DIFF_HEADS * 2 * DIFF_DH, DIFF_HEADS * 2 * DIFF_DH,
              DIFF_HEADS * DIFF_DV, D_MODEL, D_MODEL)
    cuts = np.cumsum(widths)[:-1].tolist()
    q_down, kv_down, kr, dq, dk, dv, ga, gb = jnp.split(z, cuts, axis=-1)
    q = (_rmsnorm(q_down, g_q) @ w_uq).reshape(B, S, MLA_HEADS, MLA_NOPE + MLA_ROPE)
    q_rope = _rope(q[..., MLA_NOPE:], pos)
    q_lat = jnp.einsum('bshn,chn->bshc', q[..., :MLA_NOPE], w_uk)
    c_kv = _rmsnorm(kv_down, g_kv)
    k_rope = _rope(kr[:, :, None, :], pos)[:, :, 0]
    lat = jnp.concatenate([c_kv, k_rope], axis=-1)
    dq = dq.reshape(B, S, DIFF_HEADS, 2 * DIFF_DH)
    dk = dk.reshape(B, S, DIFF_HEADS, 2 * DIFF_DH)
    dv = dv.reshape(B, S, DIFF_HEADS, DIFF_DV)
    return q_lat, q_rope, lat, dq, dk, dv, jax.nn.sigmoid(ga), jax.nn.sigmoid(gb)


def _mla_logits(q_lat, q_rope, lat):
    s = (jnp.einsum('bqhc,bkc->bhqk', q_lat, lat[..., :KV_LORA])
         + jnp.einsum('bqhr,bkr->bhqk', q_rope, lat[..., KV_LORA:]))
    return s.astype(jnp.float32) * MLA_SCALE


def _diff_logits(dq, dk, bias):
    s1 = jnp.einsum('bqhd,bkhd->bhqk', dq[..., :DIFF_DH], dk[..., :DIFF_DH]).astype(jnp.float32) * DIFF_SCALE + bias
    s2 = jnp.einsum('bqhd,bkhd->bhqk', dq[..., DIFF_DH:], dk[..., DIFF_DH:]).astype(jnp.float32) * DIFF_SCALE + bias
    return s1, s2


def _online(state, s, v, eq):
    m, l, acc = state
    m_new = jnp.maximum(m, jnp.max(s, axis=-1))
    alpha = jnp.exp(m - m_new)
    p = jnp.exp(s - m_new[..., None])
    return (m_new, l * alpha + jnp.sum(p, axis=-1),
            acc * alpha[..., None] + jnp.einsum(eq, p, v.astype(jnp.float32)))


def _attend_prompt(q_lat, q_rope, lat, dq, dk, dv, rel_bias, lam):
    B, S = q_lat.shape[:2]
    k_pos = jnp.arange(S, dtype=jnp.int32)
    c_kv = lat[..., :KV_LORA].astype(jnp.float32)
    v = dv.astype(jnp.float32)

    def block(q0):
        q_pos = q0 + jnp.arange(Q_BLOCK, dtype=jnp.int32)
        mask = k_pos[None, :] <= q_pos[:, None]
        sl = lambda a: lax.dynamic_slice_in_dim(a, q0, Q_BLOCK, axis=1)
        pa = jax.nn.softmax(jnp.where(mask, _mla_logits(sl(q_lat), sl(q_rope), lat), NEG_INF), axis=-1)
        oa = jnp.einsum('bhqk,bkc->bhqc', pa, c_kv)
        s1, s2 = _diff_logits(sl(dq), dk, _t5_bias(rel_bias, q_pos, k_pos))
        p1 = jax.nn.softmax(jnp.where(mask, s1, NEG_INF), axis=-1)
        p2 = jax.nn.softmax(jnp.where(mask, s2, NEG_INF), axis=-1)
        ob = jnp.einsum('bhqk,bkhd->bhqd', p1 - lam * p2, v)
        return oa, ob

    oa, ob = lax.map(block, jnp.arange(S // Q_BLOCK, dtype=jnp.int32) * Q_BLOCK)
    oa = jnp.transpose(oa, (1, 0, 3, 2, 4)).reshape(B, S, MLA_HEADS, KV_LORA)
    ob = jnp.transpose(ob, (1, 0, 3, 2, 4)).reshape(B, S, DIFF_HEADS, DIFF_DV)
    return oa, ob


def _attend_sample(q_lat, q_rope, lat, dq, dk, dv, cache_mla, cache_k, cache_v, page_table, layer, rel_bias, lam):
    B, T = q_lat.shape[:2]
    n_pages = PAST_LEN // PAGE_SIZE
    q_pos = PAST_LEN + jnp.arange(T, dtype=jnp.int32)
    mask = q_pos[None, :] <= q_pos[:, None]

    def init(h, d):
        return (jnp.full((B, h, T), NEG_INF, jnp.float32), jnp.zeros((B, h, T), jnp.float32),
                jnp.zeros((B, h, T, d), jnp.float32))

    st_a = _online(init(MLA_HEADS, KV_LORA), jnp.where(mask, _mla_logits(q_lat, q_rope, lat), NEG_INF),
                   lat[..., :KV_LORA], 'bhqk,bkc->bhqc')
    s1, s2 = _diff_logits(dq, dk, _t5_bias(rel_bias, q_pos, q_pos))
    st_1 = _online(init(DIFF_HEADS, DIFF_DV), jnp.where(mask, s1, NEG_INF), dv, 'bhqk,bkhd->bhqd')
    st_2 = _online(init(DIFF_HEADS, DIFF_DV), jnp.where(mask, s2, NEG_INF), dv, 'bhqk,bkhd->bhqd')

    def page_step(carry, xs):
        st_a, st_1, st_2 = carry
        j, pages = xs
        k_pos = j * PAGE_SIZE + jnp.arange(PAGE_SIZE, dtype=jnp.int32)
        lat_p = cache_mla[layer, pages]
        k_p = cache_k[layer, pages]
        v_p = cache_v[layer, pages]
        st_a = _online(st_a, _mla_logits(q_lat, q_rope, lat_p), lat_p[..., :KV_LORA], 'bhqk,bkc->bhqc')
        s1, s2 = _diff_logits(dq, k_p, _t5_bias(rel_bias, q_pos, k_pos))
        st_1 = _online(st_1, s1, v_p, 'bhqk,bkhd->bhqd')
        st_2 = _online(st_2, s2, v_p, 'bhqk,bkhd->bhqd')
        return (st_a, st_1, st_2), None

    (st_a, st_1, st_2), _ = lax.scan(page_step, (st_a, st_1, st_2),
                                     (jnp.arange(n_pages, dtype=jnp.int32), page_table.T))
    oa = st_a[2] / st_a[1][..., None]
    ob = st_1[2] / st_1[1][..., None] - lam * (st_2[2] / st_2[1][..., None])
    return jnp.transpose(oa, (0, 2, 1, 3)), jnp.transpose(ob, (0, 2, 1, 3))


def _mixer_out(oa_lat, ob, ga, gb, w_uv, g_subln, lam_init, w_a, w_b, w_o):
    B, S = ga.shape[:2]
    dt = ga.dtype
    oa = jnp.einsum('bshc,chv->bshv', oa_lat.astype(dt), w_uv).reshape(B, S, BRANCH_A)
    ob = (_rmsnorm(ob, g_subln) * (1.0 - lam_init)).astype(dt).reshape(B, S, BRANCH_B)
    return (ga * (oa @ w_a) + gb * (ob @ w_b)) @ w_o


def _moe(h, w_router, router_bias, w_exp_gu, w_exp_down, w_sh_gu, w_sh_down):
    B, S, D = h.shape
    x = h.reshape(B * S, D)
    aff = jax.nn.sigmoid((x @ w_router).astype(jnp.float32))
    sel = aff + router_bias.astype(jnp.float32)
    grp_score = jnp.sum(lax.top_k(sel.reshape(-1, N_GROUPS, N_EXPERTS // N_GROUPS), 2)[0], axis=-1)
    _, gidx = lax.top_k(grp_score, TOPK_GROUPS)
    gmask = jnp.sum(jax.nn.one_hot(gidx, N_GROUPS, dtype=jnp.float32), axis=1)
    emask = jnp.repeat(gmask, N_EXPERTS // N_GROUPS, axis=-1) > 0
    _, eidx = lax.top_k(jnp.where(emask, sel, -jnp.inf), TOP_K)
    w = jnp.take_along_axis(aff, eidx, axis=-1)
    w = w / jnp.sum(w, axis=-1, keepdims=True) * ROUTE_SCALE
    gates = jnp.sum(jax.nn.one_hot(eidx, N_EXPERTS, dtype=jnp.float32) * w[..., None], axis=1).astype(h.dtype)

    def expert(acc, xs):
        wgu, wd, g = xs
        gu = x @ wgu
        hid = jax.nn.silu(gu[:, :D_EXPERT]) * gu[:, D_EXPERT:]
        return acc + g[:, None] * (hid @ wd), None

    y, _ = lax.scan(expert, jnp.zeros_like(x), (w_exp_gu, w_exp_down, gates.T))
    sgu = x @ w_sh_gu
    y = y + (jax.nn.silu(sgu[:, :D_SHARED]) * sgu[:, D_SHARED:]) @ w_sh_down
    return y.reshape(B, S, D)


def setup_inputs(seed: int = 0) -> dict:
    key = jax.random.key(seed)
    keys = jax.random.split(key, 48)
    counter = [0]

    def nk():
        k = keys[counter[0]]
        counter[0] += 1
        return k

    def nrm(shape, scale):
        return jax.random.normal(nk(), shape, jnp.float32) * scale

    def gain(shape):
        return 1.0 + 0.01 * jax.random.normal(nk(), shape, jnp.float32)

    L, D = DEPTH, D_MODEL
    n_pages = PAST_LEN // PAGE_SIZE
    n_used = DEC_BATCH * n_pages
    n_pool = n_used + max(1, n_used // 4)
    page_table = jax.random.permutation(nk(), n_pool)[:n_used].reshape(DEC_BATCH, n_pages).astype(jnp.int32)
    return {
        'x_prompt': nrm((BATCH, SEQ, D), 1.0),
        'x_sample': nrm((DEC_BATCH, DEC_SEQ, D), 1.0),
        'c_prompt': nrm((BATCH, D), 1.0),
        'c_sample': nrm((DEC_BATCH, D), 1.0),
        'cache_mla': nrm((L, n_pool, PAGE_SIZE, MLA_ROW), 1.0),
        'cache_k': nrm((L, n_pool, PAGE_SIZE, DIFF_HEADS, 2 * DIFF_DH), 1.0),
        'cache_v': nrm((L, n_pool, PAGE_SIZE, DIFF_HEADS, DIFF_DV), 1.0),
        'page_table': page_table,
        'w_ada': nrm((L, D, 6 * D), 0.5 * D ** -0.5),
        'b_ada': nrm((L, 6 * D), 0.01),
        'g_mix': gain((L, D)),
        'w_in': nrm((L, D, D_IN), D ** -0.5),
        'g_q': gain((L, Q_LORA)),
        'w_uq': nrm((L, Q_LORA, MLA_HEADS * (MLA_NOPE + MLA_ROPE)), Q_LORA ** -0.5),
        'g_kv': gain((L, KV_LORA)),
        'w_uk': nrm((L, KV_LORA, MLA_HEADS, MLA_NOPE), KV_LORA ** -0.5),
        'w_uv': nrm((L, KV_LORA, MLA_HEADS, MLA_V), KV_LORA ** -0.5),
        'lam_q1': nrm((L, DIFF_DH), 0.1),
        'lam_k1': nrm((L, DIFF_DH), 0.1),
        'lam_q2': nrm((L, DIFF_DH), 0.1),
        'lam_k2': nrm((L, DIFF_DH), 0.1),
        'g_subln': gain((L, DIFF_DV)),
        'w_a': nrm((L, BRANCH_A, D), BRANCH_A ** -0.5),
        'w_b': nrm((L, BRANCH_B, D), BRANCH_B ** -0.5),
        'w_o': nrm((L, D, D), D ** -0.5),
        'rel_bias': nrm((N_BUCKETS, DIFF_HEADS), 0.5),
        'g_ffn': gain((L, D)),
        'w_router': nrm((L, D, N_EXPERTS), D ** -0.5),
        'router_bias': nrm((L, N_EXPERTS), 0.01),
        'w_exp_gu': nrm((L, N_EXPERTS, D, 2 * D_EXPERT), D ** -0.5),
        'w_exp_down': nrm((L, N_EXPERTS, D_EXPERT, D), D_EXPERT ** -0.5),
        'w_sh_gu': nrm((L, D, 2 * D_SHARED), D ** -0.5),
        'w_sh_down': nrm((L, D_SHARED, D), D_SHARED ** -0.5),
        'g_final': gain((D,)),
    }


def reference(x_prompt, x_sample, c_prompt, c_sample, cache_mla, cache_k, cache_v, page_table,
              w_ada, b_ada, g_mix, w_in, g_q, w_uq, g_kv, w_uk, w_uv,
              lam_q1, lam_k1, lam_q2, lam_k2, g_subln, w_a, w_b, w_o, rel_bias,
              g_ffn, w_router, router_bias, w_exp_gu, w_exp_down, w_sh_gu, w_sh_down, g_final):
    f32 = jnp.float32
    xp, xs = x_prompt, x_sample
    pos_p = jnp.arange(x_prompt.shape[1], dtype=jnp.int32)
    pos_s = PAST_LEN + jnp.arange(x_sample.shape[1], dtype=jnp.int32)
    mla_p, k_p, v_p, mla_s, k_s, v_s = [], [], [], [], [], []
    for l in range(DEPTH):
        lam_init = 0.8 - 0.6 * math.exp(-0.3 * l)
        lam = (jnp.exp(jnp.sum(lam_q1[l].astype(f32) * lam_k1[l].astype(f32)))
               - jnp.exp(jnp.sum(lam_q2[l].astype(f32) * lam_k2[l].astype(f32))) + lam_init)

        def layer(x, c, pos, attend):
            mod = (jax.nn.silu(c) @ w_ada[l] + b_ada[l])[:, None, :]
            sh1, sc1, gt1, sh2, sc2, gt2 = jnp.split(mod, 6, axis=-1)
            h = _rmsnorm(x, g_mix[l]) * (1 + sc1) + sh1
            q_lat, q_rope, lat, dq, dk, dv, ga, gb = _mixer_inputs(h, pos, w_in[l], g_q[l], w_uq[l], g_kv[l], w_uk[l])
            oa, ob = attend(q_lat, q_rope, lat, dq, dk, dv)
            x = x + gt1 * _mixer_out(oa, ob, ga, gb, w_uv[l], g_subln[l], lam_init, w_a[l], w_b[l], w_o[l])
            h2 = _rmsnorm(x, g_ffn[l]) * (1 + sc2) + sh2
            x = x + gt2 * _moe(h2, w_router[l], router_bias[l], w_exp_gu[l], w_exp_down[l], w_sh_gu[l], w_sh_down[l])
            return x, lat, dk, dv

        xp, lat, dk, dv = layer(xp, c_prompt, pos_p,
                                lambda *a: _attend_prompt(*a, rel_bias, lam))
        mla_p.append(lat)
        k_p.append(dk)
        v_p.append(dv)
        xs, lat, dk, dv = layer(xs, c_sample, pos_s,
                                lambda *a: _attend_sample(*a, cache_mla, cache_k, cache_v, page_table, l, rel_bias, lam))
        mla_s.append(lat)
        k_s.append(dk)
        v_s.append(dv)
    y_prompt = _rmsnorm(xp, g_final)
    y_sample = _rmsnorm(xs, g_final)
    return (y_prompt, y_sample, jnp.stack(mla_p), jnp.stack(k_p), jnp.stack(v_p),
            jnp.stack(mla_s), jnp.stack(k_s), jnp.stack(v_s))
```

```python
import functools
import math

import jax
import jax.numpy as jnp
from jax import lax
from jax.experimental import pallas as pl
from jax.experimental.pallas import tpu as pltpu

F32 = jnp.float32
BF16 = jnp.bfloat16

D_MODEL = 1024
PAGE = 128
MLA_HEADS = 8
MLA_NOPE = 64
MLA_ROPE = 32
Q_LORA = 256
KV_LORA = 128
MLA_ROW = KV_LORA + MLA_ROPE
MLA_SCALE = (MLA_NOPE + MLA_ROPE) ** -0.5
ROPE_THETA = 10000.0
DIFF_HEADS = 4
DIFF_DH = 64
DIFF_DV = 128
DIFF_SCALE = DIFF_DH ** -0.5
N_BUCKETS = 32
MAX_DISTANCE = 128
N_EXPERTS = 64
N_GROUPS = 8
TOPK_GROUPS = 4
TOP_K = 8
D_EXPERT = 256
D_SHARED = 256
ROUTE_SCALE = 2.5
EPS = 1e-6
NEG = -1e30
LAM_INIT = 0.8 - 0.6 * math.exp(-0.3 * 0)

QW = 256
W_IN_COLS = 4096
BQ = 256
VMEM_LIMIT = 56 * 1024 * 1024


def _cp(*sem):
    return pltpu.CompilerParams(dimension_semantics=sem, vmem_limit_bytes=VMEM_LIMIT)


def _dot(a, b):
    return jnp.dot(a, b, preferred_element_type=F32)


def _dot_nt(a, b):
    return lax.dot_general(a, b, (((1,), (1,)), ((), ())), preferred_element_type=F32)


def _rms(x, g):
    return x * lax.rsqrt(jnp.mean(x * x, axis=-1, keepdims=True) + EPS) * g


def _silu(x):
    return x * jax.nn.sigmoid(x)


def _lam(lq1_ref, lk1_ref, lq2_ref, lk2_ref):
    a = jnp.sum(lq1_ref[...] * lk1_ref[...], axis=-1, keepdims=True)
    b = jnp.sum(lq2_ref[...] * lk2_ref[...], axis=-1, keepdims=True)
    return jnp.exp(a) - jnp.exp(b) + LAM_INIT


def _bmm_kernel(a_ref, b_ref, o_ref):
    o_ref[...] = jnp.dot(a_ref[...], b_ref[...], preferred_element_type=F32, precision=lax.Precision.HIGHEST)


def _bmm(a, b):
    h, m, k = a.shape
    n = b.shape[2]
    return pl.pallas_call(
        _bmm_kernel,
        grid=(h,),
        in_specs=[pl.BlockSpec((None, m, k), lambda i: (i, 0, 0)), pl.BlockSpec((None, k, n), lambda i: (i, 0, 0))],
        out_specs=pl.BlockSpec((None, m, n), lambda i: (i, 0, 0)),
        out_shape=jax.ShapeDtypeStruct((h, m, n), F32),
        compiler_params=_cp("parallel"),
        name="fold_weights",
    )(a, b)


def _ada_kernel(c_ref, w_ref, b_ref, o_ref):
    c = c_ref[...]
    o_ref[...] = _dot(_silu(c).astype(BF16), w_ref[...].astype(BF16)) + b_ref[...]


def _ada(c, w, b):
    n = c.shape[0]
    return pl.pallas_call(
        _ada_kernel,
        grid=(6,),
        in_specs=[pl.BlockSpec((n, D_MODEL), lambda j: (0, 0)),
                  pl.BlockSpec((D_MODEL, D_MODEL), lambda j: (0, j)),
                  pl.BlockSpec((1, D_MODEL), lambda j: (0, j))],
        out_specs=pl.BlockSpec((n, D_MODEL), lambda j: (0, j)),
        out_shape=jax.ShapeDtypeStruct((n, 6 * D_MODEL), F32),
        compiler_params=_cp("parallel"),
        name="adaln",
    )(c, w, b.reshape(1, -1))


def _mod_spec(r, tiles_per_group, k, grid_rank=1):
    if grid_rank == 1:
        return pl.BlockSpec((None, r, D_MODEL), lambda i: (i // tiles_per_group, 0, k))
    return pl.BlockSpec((None, r, D_MODEL), lambda i, e: (i // tiles_per_group, 0, k))


def _rope128(v, c, s1, s2):
    return v * c + pltpu.roll(v, 112, 1) * s1 + pltpu.roll(v, 16, 1) * s2


def _in_kernel(x_ref, sh1_ref, sc1_ref, gmix_ref, win_ref, gq_ref, wq_ref, gkv_ref, c_ref, s1_ref, s2_ref,
               q_ref, kcat_ref, ckvt_ref, latt_ref, dq_ref, dk_ref, dv_ref, dkb_ref, dvt_ref, ga_ref, gb_ref):
    tm = x_ref.shape[0]
    x = x_ref[...]
    h = _rms(x, gmix_ref[...]) * (1.0 + sc1_ref[...]) + sh1_ref[...]
    hb = h.astype(BF16)
    c, s1, s2 = c_ref[...], s1_ref[...], s2_ref[...]

    qn = _rms(_dot(hb, win_ref[:, 0:256]), gq_ref[...]).astype(BF16)
    for hd in range(MLA_HEADS):
        qh = _dot(qn, wq_ref[:, hd * QW:(hd + 1) * QW])
        q_ref[:, hd * QW:hd * QW + 128] = qh[:, 0:128].astype(BF16)
        q_ref[:, hd * QW + 128:(hd + 1) * QW] = _rope128(qh[:, 128:256], c, s1, s2).astype(BF16)

    ckv = _rms(_dot(hb, win_ref[:, 256:384]), gkv_ref[...])
    kr = _rope128(_dot(hb, win_ref[:, 384:512]), c, s1, s2)
    ckv_t = ckv.T
    latt_ref[0:KV_LORA, :] = ckv_t
    latt_ref[KV_LORA:MLA_ROW, :] = kr.T[0:MLA_ROPE, :]
    kcat_ref[:, 0:128] = ckv.astype(BF16)
    kcat_ref[:, 128:256] = kr.astype(BF16)
    for cc in range(tm // BQ):
        ckvt_ref[cc] = ckv_t[:, cc * BQ:(cc + 1) * BQ].astype(BF16)

    dq_ref[...] = (_dot(hb, win_ref[:, 512:1024]) * DIFF_SCALE).astype(BF16)
    dk = _dot(hb, win_ref[:, 1024:1536])
    dkb_ref[...] = dk.astype(BF16)
    dv = _dot(hb, win_ref[:, 1536:2048])
    for hd in range(DIFF_HEADS):
        rows = pl.ds(hd, tm, stride=DIFF_HEADS)
        dk_ref[rows, :] = dk[:, hd * 128:(hd + 1) * 128]
        dv_ref[rows, :] = dv[:, hd * 128:(hd + 1) * 128]
    for cc in range(tm // BQ):
        for hd in range(DIFF_HEADS):
            dvt_ref[cc, hd] = dv[cc * BQ:(cc + 1) * BQ, hd * 128:(hd + 1) * 128].T.astype(BF16)
    ga_ref[...] = jax.nn.sigmoid(_dot(hb, win_ref[:, 2048:3072])).astype(BF16)
    gb_ref[...] = jax.nn.sigmoid(_dot(hb, win_ref[:, 3072:4096])).astype(BF16)


def _mixer_in(x, mod3, tiles_per_group, tabs, tab_blocks, wts, tm):
    t = x.shape[0]
    r = mod3.shape[1]
    nt = t // tm
    c_t, s1_t, s2_t = tabs
    g_mix, w_in_p, g_q, w_q, g_kv = wts
    const = lambda i: (0, 0)
    tab_spec = pl.BlockSpec((tm, 128), lambda i: (i % tab_blocks, 0))
    row = lambda w: pl.BlockSpec((tm, w), lambda i: (i, 0))
    nck = tm // BQ
    out_shape = (
        jax.ShapeDtypeStruct((t, MLA_HEADS * QW), BF16),
        jax.ShapeDtypeStruct((t, QW), BF16),
        jax.ShapeDtypeStruct((t // BQ, 128, BQ), BF16),
        jax.ShapeDtypeStruct((t // (tab_blocks * tm), MLA_ROW, tab_blocks * tm), F32),
        jax.ShapeDtypeStruct((t, 512), BF16),
        jax.ShapeDtypeStruct((t * DIFF_HEADS, 128), F32),
        jax.ShapeDtypeStruct((t * DIFF_HEADS, 128), F32),
        jax.ShapeDtypeStruct((t, 512), BF16),
        jax.ShapeDtypeStruct((t // BQ, DIFF_HEADS, 128, BQ), BF16),
        jax.ShapeDtypeStruct((t, D_MODEL), BF16),
        jax.ShapeDtypeStruct((t, D_MODEL), BF16),
    )
    out_specs = (
        row(MLA_HEADS * QW), row(QW),
        pl.BlockSpec((nck, 128, BQ), lambda i: (i, 0, 0)),
        pl.BlockSpec((None, MLA_ROW, tm), lambda i: (i // tab_blocks, 0, i % tab_blocks)),
        row(512),
        pl.BlockSpec((tm * DIFF_HEADS, 128), lambda i: (i, 0)),
        pl.BlockSpec((tm * DIFF_HEADS, 128), lambda i: (i, 0)),
        row(512),
        pl.BlockSpec((nck, DIFF_HEADS, 128, BQ), lambda i: (i, 0, 0, 0)),
        row(D_MODEL), row(D_MODEL),
    )
    return pl.pallas_call(
        _in_kernel,
        grid=(nt,),
        in_specs=[row(D_MODEL), _mod_spec(r, tiles_per_group, 0), _mod_spec(r, tiles_per_group, 1),
                  pl.BlockSpec((1, D_MODEL), const),
                  pl.BlockSpec((D_MODEL, W_IN_COLS), const),
                  pl.BlockSpec((1, Q_LORA), const),
                  pl.BlockSpec((Q_LORA, MLA_HEADS * QW), const),
                  pl.BlockSpec((1, KV_LORA), const),
                  tab_spec, tab_spec, tab_spec],
        out_specs=out_specs,
        out_shape=out_shape,
        compiler_params=_cp("parallel"),
        name="mixer_in",
    )(x, mod3, mod3, g_mix, w_in_p, g_q, w_q, g_kv, c_t, s1_t, s2_t)


def _attn_prompt_kernel(q_ref, kc_ref, ckvt_ref, dq_ref, dk_ref, dvt_ref, bias_ref,
                        lq1_ref, lk1_ref, lq2_ref, lk2_ref, gsub_ref,
                        oa_ref, ob_ref, m_a, l_a, acc_a, m_d, l_d, acc_d, qbd):
    i = pl.program_id(1)
    m_a[...] = jnp.full(m_a.shape, NEG, F32)
    l_a[...] = jnp.zeros(l_a.shape, F32)
    acc_a[...] = jnp.zeros(acc_a.shape, F32)
    m_d[...] = jnp.full(m_d.shape, NEG, F32)
    l_d[...] = jnp.zeros(l_d.shape, F32)
    acc_d[...] = jnp.zeros(acc_d.shape, F32)
    lane = lax.broadcasted_iota(jnp.int32, (BQ, 128), 1)
    for hd in range(DIFF_HEADS):
        dqh = dq_ref[:, hd * 128:(hd + 1) * 128].astype(F32)
        qbd[hd, 0:BQ, :] = jnp.where(lane < DIFF_DH, dqh, 0.0).astype(BF16)
        qbd[hd, BQ:2 * BQ, :] = jnp.where(lane >= DIFF_DH, dqh, 0.0).astype(BF16)
    krow = lax.broadcasted_iota(jnp.int32, (BQ, BQ), 0)
    qcol = lax.broadcasted_iota(jnp.int32, (BQ, BQ), 1)

    def body(j, carry):
        k0 = pl.multiple_of(j * BQ, BQ)
        mask = (krow + j * BQ) <= (qcol + i * BQ)
        kc = kc_ref[pl.ds(k0, BQ), :]
        vt = ckvt_ref[j]
        for hd in range(MLA_HEADS):
            s = _dot_nt(kc, q_ref[:, hd * QW:(hd + 1) * QW])
            s = jnp.where(mask, s, NEG)
            m_prev = m_a[hd:hd + 1, :]
            m_new = jnp.maximum(m_prev, jnp.max(s, axis=0, keepdims=True))
            alpha = jnp.exp(m_prev - m_new)
            p = jnp.exp(s - m_new)
            l_a[hd:hd + 1, :] = alpha * l_a[hd:hd + 1, :] + jnp.sum(p, axis=0, keepdims=True)
            m_a[hd:hd + 1, :] = m_new
            acc_a[hd] = alpha * acc_a[hd] + _dot(vt, p.astype(BF16))
        mask2 = jnp.concatenate([mask, mask], axis=1)
        for hd in range(DIFF_HEADS):
            dkc = dk_ref[pl.ds(k0, BQ), hd * 128:(hd + 1) * 128]
            b = bias_ref[hd, i - j]
            s = _dot_nt(dkc, qbd[hd]) + jnp.concatenate([b, b], axis=1)
            s = jnp.where(mask2, s, NEG)
            m_prev = m_d[hd:hd + 1, :]
            m_new = jnp.maximum(m_prev, jnp.max(s, axis=0, keepdims=True))
            alpha = jnp.exp(m_prev - m_new)
            p = jnp.exp(s - m_new)
            l_d[hd:hd + 1, :] = alpha * l_d[hd:hd + 1, :] + jnp.sum(p, axis=0, keepdims=True)
            m_d[hd:hd + 1, :] = m_new
            acc_d[hd] = alpha * acc_d[hd] + _dot(dvt_ref[j, hd], p.astype(BF16))
        return carry

    lax.fori_loop(0, i + 1, body, 0)

    for hd in range(MLA_HEADS):
        ot = acc_a[hd] / l_a[hd:hd + 1, :]
        oa_ref[:, hd * 128:(hd + 1) * 128] = ot.T.astype(BF16)
    lam = _lam(lq1_ref, lk1_ref, lq2_ref, lk2_ref)
    for hd in range(DIFF_HEADS):
        a = acc_d[hd]
        l = l_d[hd:hd + 1, :]
        ot = a[:, 0:BQ] / l[:, 0:BQ] - lam * (a[:, BQ:2 * BQ] / l[:, BQ:2 * BQ])
        ms = jnp.mean(ot * ot, axis=0, keepdims=True)
        ot = ot * lax.rsqrt(ms + EPS) * gsub_ref[...] * (1.0 - LAM_INIT)
        ob_ref[:, hd * 128:(hd + 1) * 128] = ot.T.astype(BF16)


def _attn_prompt(q, kcat, ckvt, dq, dkb, dvt, bias_t, lam_vecs, gsub_col, batch, seq):
    t = q.shape[0]
    nq = seq // BQ
    qrow = lambda w: pl.BlockSpec((BQ, w), lambda b, i: (b * nq + i, 0))
    vec = pl.BlockSpec((1, DIFF_DH), lambda b, i: (0, 0))
    return pl.pallas_call(
        _attn_prompt_kernel,
        grid=(batch, nq),
        in_specs=[qrow(MLA_HEADS * QW),
                  pl.BlockSpec((seq, QW), lambda b, i: (b, 0)),
                  pl.BlockSpec((nq, 128, BQ), lambda b, i: (b, 0, 0)),
                  qrow(512),
                  pl.BlockSpec((seq, 512), lambda b, i: (b, 0)),
                  pl.BlockSpec((nq, DIFF_HEADS, 128, BQ), lambda b, i: (b, 0, 0, 0)),
                  pl.BlockSpec((DIFF_HEADS, nq, BQ, BQ), lambda b, i: (0, 0, 0, 0)),
                  vec, vec, vec, vec,
                  pl.BlockSpec((DIFF_DV, 1), lambda b, i: (0, 0))],
        out_specs=(qrow(MLA_HEADS * 128), qrow(512)),
        out_shape=(jax.ShapeDtypeStruct((t, MLA_HEADS * 128), BF16), jax.ShapeDtypeStruct((t, 512), BF16)),
        scratch_shapes=[pltpu.VMEM((MLA_HEADS, BQ), F32), pltpu.VMEM((MLA_HEADS, BQ), F32),
                        pltpu.VMEM((MLA_HEADS, 128, BQ), F32),
                        pltpu.VMEM((DIFF_HEADS, 2 * BQ), F32), pltpu.VMEM((DIFF_HEADS, 2 * BQ), F32),
                        pltpu.VMEM((DIFF_HEADS, 128, 2 * BQ), F32),
                        pltpu.VMEM((DIFF_HEADS, 2 * BQ, 128), BF16)],
        compiler_params=_cp("parallel", "arbitrary"),
        name="attn_prompt",
    )(q, kcat, ckvt, dq, dkb, dvt, bias_t, *lam_vecs, gsub_col)


def _attn_sample_kernel(npg, pt_ref, q_ref, dq_ref, kcn_ref, dkn_ref, dvn_ref, biasn_ref, bias_ref,
                        lq1_ref, lk1_ref, lq2_ref, lk2_ref, gsub_ref, *rest):
    lat_pages = rest[0:npg]
    k_pages = rest[npg:2 * npg]
    v_pages = rest[2 * npg:3 * npg]
    oa_ref, ob_ref = rest[3 * npg:3 * npg + 2]
    (lattb, kb, vb, pgl, pgk, pgv, qs, qbd, m_a, l_a, acc_a, m_d, l_d, acc_d) = rest[3 * npg + 2:]
    del pt_ref
    j = pl.program_id(1)
    nrow = MLA_HEADS * 8

    def update(nk, bias, mask):
        s = _dot(qs[...].astype(BF16), lattb[:, 0:nk])
        if mask is not None:
            s = jnp.where(mask, s, NEG)
        m_prev = m_a[...]
        m_new = jnp.maximum(m_prev, jnp.max(s, axis=1, keepdims=True))
        alpha = jnp.exp(m_prev - m_new)
        p = jnp.exp(s - m_new[:, 0:1])
        l_a[...] = alpha * l_a[...] + jnp.sum(p, axis=1, keepdims=True)
        m_a[...] = m_new
        acc_a[...] = alpha * acc_a[...] + _dot_nt(p.astype(BF16), lattb[0:KV_LORA, 0:nk])
        s = _dot_nt(qbd[...].astype(BF16), kb[0:nk, :]) + bias
        if mask is not None:
            s = jnp.where(mask, s, NEG)
        m_prev = m_d[...]
        m_new = jnp.maximum(m_prev, jnp.max(s, axis=1, keepdims=True))
        alpha = jnp.exp(m_prev - m_new)
        p = jnp.exp(s - m_new[:, 0:1])
        l_d[...] = alpha * l_d[...] + jnp.sum(p, axis=1, keepdims=True)
        m_d[...] = m_new
        acc_d[...] = alpha[:, 0:1] * acc_d[...] + _dot(p.astype(BF16), vb[0:nk, :])

    @pl.when(j == 0)
    def _():
        m_a[...] = jnp.full(m_a.shape, NEG, F32)
        l_a[...] = jnp.zeros(l_a.shape, F32)
        acc_a[...] = jnp.zeros(acc_a.shape, F32)
        m_d[...] = jnp.full(m_d.shape, NEG, F32)
        l_d[...] = jnp.zeros(l_d.shape, F32)
        acc_d[...] = jnp.zeros(acc_d.shape, F32)
        lane = lax.broadcasted_iota(jnp.int32, (8, 512), 1)
        dq = dq_ref[...].astype(F32)
        for hd in range(MLA_HEADS):
            qs[hd * 8:(hd + 1) * 8, :] = q_ref[:, hd * QW:(hd + 1) * QW].astype(F32)
        for hd in range(DIFF_HEADS):
            for mp in range(2):
                lo = hd * 128 + mp * DIFF_DH
                r0 = (hd * 2 + mp) * 8
                qbd[r0:r0 + 8, :] = jnp.where((lane >= lo) & (lane < lo + DIFF_DH), dq, 0.0)
        pgl[...] = jnp.zeros(pgl.shape, F32)
        pgk[...] = jnp.zeros(pgk.shape, F32)
        pgv[...] = jnp.zeros(pgv.shape, F32)
        pgl[0:8, :] = kcn_ref[...].astype(F32)
        pgk[0:8, :] = dkn_ref[...]
        pgv[0:8, :] = dvn_ref[...]
        lattb[:, 0:PAGE] = pgl[...].T.astype(BF16)
        kb[0:PAGE, :] = pgk[...].astype(BF16)
        vb[0:PAGE, :] = pgv[...].astype(BF16)
        row = lax.broadcasted_iota(jnp.int32, (nrow, PAGE), 0)
        col = lax.broadcasted_iota(jnp.int32, (nrow, PAGE), 1)
        update(PAGE, biasn_ref[...], col <= (row % 8))
        lattb[...] = jnp.zeros(lattb.shape, BF16)

    for u in range(npg):
        lattb[0:MLA_ROW, u * PAGE:(u + 1) * PAGE] = lat_pages[u][...].astype(BF16)
        for hd in range(DIFF_HEADS):
            rows = pl.ds(hd, PAGE, stride=DIFF_HEADS)
            kb[u * PAGE:(u + 1) * PAGE, hd * 128:(hd + 1) * 128] = k_pages[u][rows, :].astype(BF16)
            vb[u * PAGE:(u + 1) * PAGE, hd * 128:(hd + 1) * 128] = v_pages[u][rows, :].astype(BF16)
    update(npg * PAGE, bias_ref[...], None)

    @pl.when(j == pl.num_programs(1) - 1)
    def _():
        oa = acc_a[...] / l_a[...]
        for hd in range(MLA_HEADS):
            oa_ref[:, hd * 128:(hd + 1) * 128] = oa[hd * 8:(hd + 1) * 8, :].astype(BF16)
        lam = _lam(lq1_ref, lk1_ref, lq2_ref, lk2_ref)
        od = acc_d[...] / l_d[:, 0:1]
        for hd in range(DIFF_HEADS):
            o1 = od[hd * 16:hd * 16 + 8, hd * 128:(hd + 1) * 128]
            o2 = od[hd * 16 + 8:hd * 16 + 16, hd * 128:(hd + 1) * 128]
            o = _rms(o1 - lam * o2, gsub_ref[...]) * (1.0 - LAM_INIT)
            ob_ref[:, hd * 128:(hd + 1) * 128] = o.astype(BF16)


def _attn_sample(q, dq, kcat, dk, dv, bias_new, bias_cache, lam_vecs, gsub_row, cache_mla_t, cache_k, cache_v,
                 page_table, npg):
    nb = page_table.shape[0]
    nsteps = page_table.shape[1] // npg
    nk = npg * PAGE
    q3 = q.reshape(nb, 8, MLA_HEADS * QW)
    dq3 = dq.reshape(nb, 8, 512)
    kc3 = kcat.reshape(nb, 8, QW)
    dk3 = dk.reshape(nb, 8, 512)
    dv3 = dv.reshape(nb, 8, 512)
    prow = PAGE * DIFF_HEADS
    ck = cache_k.reshape(-1, 128)
    cv = cache_v.reshape(-1, 128)
    seqblk = lambda w: pl.BlockSpec((None, 8, w), lambda b, j, pt: (b, 0, 0))
    vec = pl.BlockSpec((1, DIFF_DH), lambda b, j, pt: (0, 0))

    def lat_spec(u):
        return pl.BlockSpec((None, MLA_ROW, PAGE), lambda b, j, pt: (pt[b, j * npg + u], 0, 0))

    def kv_spec(u):
        return pl.BlockSpec((prow, 128), lambda b, j, pt: (pt[b, j * npg + u], 0))

    in_specs = [seqblk(MLA_HEADS * QW), seqblk(512), seqblk(QW), seqblk(512), seqblk(512),
                pl.BlockSpec((64, PAGE), lambda b, j, pt: (0, 0)),
                pl.BlockSpec((64, nk), lambda b, j, pt: (0, j)),
                vec, vec, vec, vec,
                pl.BlockSpec((1, DIFF_DV), lambda b, j, pt: (0, 0))]
    in_specs += [lat_spec(u) for u in range(npg)]
    in_specs += [kv_spec(u) for u in range(npg)]
    in_specs += [kv_spec(u) for u in range(npg)]
    grid_spec = pltpu.PrefetchScalarGridSpec(
        num_scalar_prefetch=1,
        grid=(nb, nsteps),
        in_specs=in_specs,
        out_specs=(seqblk(MLA_HEADS * 128), seqblk(512)),
        scratch_shapes=[pltpu.VMEM((QW, nk), BF16), pltpu.VMEM((nk, 512), BF16), pltpu.VMEM((nk, 512), BF16),
                        pltpu.VMEM((PAGE, QW), F32), pltpu.VMEM((PAGE, 512), F32), pltpu.VMEM((PAGE, 512), F32),
                        pltpu.VMEM((64, QW), F32), pltpu.VMEM((64, 512), F32),
                        pltpu.VMEM((64, 128), F32), pltpu.VMEM((64, 128), F32), pltpu.VMEM((64, 128), F32),
                        pltpu.VMEM((64, 128), F32), pltpu.VMEM((64, 128), F32), pltpu.VMEM((64, 512), F32)],
    )
    oa, ob = pl.pallas_call(
        functools.partial(_attn_sample_kernel, npg),
        grid_spec=grid_spec,
        out_shape=(jax.ShapeDtypeStruct((nb, 8, MLA_HEADS * 128), BF16), jax.ShapeDtypeStruct((nb, 8, 512), BF16)),
        compiler_params=_cp("parallel", "arbitrary"),
        name="attn_sample",
    )(page_table, q3, dq3, kc3, dk3, dv3, bias_new, bias_cache, *lam_vecs, gsub_row,
      *([cache_mla_t] * npg), *([ck] * npg), *([cv] * npg))
    return oa.reshape(nb * 8, MLA_HEADS * 128), ob.reshape(nb * 8, 512)


def _first_max(v, idx, big):
    m = jnp.max(v, axis=0, keepdims=True)
    f = jnp.min(jnp.where(v == m, idx, big), axis=0, keepdims=True)
    return m, idx == f


def _mix_out_kernel(x_ref, oa_ref, ob_ref, ga_ref, gb_ref, gt1_ref, sh2_ref, sc2_ref, wova_ref, wb_ref, wo_ref, gffn_ref,
                    wrt_ref, rb_ref, x1_ref, h2_ref, gates_ref):
    tm = x_ref.shape[0]
    a = (ga_ref[...].astype(F32) * _dot(oa_ref[...], wova_ref[...])
         + gb_ref[...].astype(F32) * _dot(ob_ref[...], wb_ref[...]))
    x1 = x_ref[...] + gt1_ref[...] * _dot(a.astype(BF16), wo_ref[...])
    x1_ref[...] = x1
    h2 = _rms(x1, gffn_ref[...]) * (1.0 + sc2_ref[...]) + sh2_ref[...]
    h2_ref[...] = h2.astype(BF16)

    logits = lax.dot_general(wrt_ref[...], h2, (((1,), (1,)), ((), ())), preferred_element_type=F32,
                             precision=lax.Precision.HIGHEST)
    aff = jax.nn.sigmoid(logits)
    sel = aff + rb_ref[...]
    gsz = N_EXPERTS // N_GROUPS
    idx8 = lax.broadcasted_iota(jnp.int32, (gsz, tm), 0).astype(F32)
    scores = []
    for g in range(N_GROUPS):
        v = sel[g * gsz:(g + 1) * gsz, :]
        m1, hit = _first_max(v, idx8, float(gsz))
        m2 = jnp.max(jnp.where(hit, -jnp.inf, v), axis=0, keepdims=True)
        scores.append(m1 + m2)
    gs = jnp.concatenate(scores, axis=0)
    gidx = lax.broadcasted_iota(jnp.int32, (N_GROUPS, tm), 0).astype(F32)
    gsel = jnp.zeros((N_GROUPS, tm), F32)
    for _ in range(TOPK_GROUPS):
        _, hit = _first_max(gs, gidx, float(N_GROUPS))
        gsel = jnp.where(hit, 1.0, gsel)
        gs = jnp.where(hit, -jnp.inf, gs)
    cand = jnp.concatenate(
        [jnp.where(gsel[g:g + 1, :] > 0.0, sel[g * gsz:(g + 1) * gsz, :], -jnp.inf) for g in range(N_GROUPS)], axis=0)
    eidx = lax.broadcasted_iota(jnp.int32, (N_EXPERTS, tm), 0).astype(F32)
    w = jnp.zeros((N_EXPERTS, tm), F32)
    for _ in range(TOP_K):
        _, hit = _first_max(cand, eidx, float(N_EXPERTS))
        w = jnp.where(hit, aff, w)
        cand = jnp.where(hit, -jnp.inf, cand)
    gates_t = w / jnp.sum(w, axis=0, keepdims=True) * ROUTE_SCALE
    gates_ref[...] = jnp.concatenate([gates_t, jnp.zeros((128 - N_EXPERTS, tm), F32)], axis=0).T


def _mix_out(x, oa, ob, ga, gb, mod3, tiles_per_group, wts, tm):
    t = x.shape[0]
    r = mod3.shape[1]
    w_ova, w_b, w_o, g_ffn, w_rt, rb = wts
    const = lambda i: (0, 0)
    row = lambda w: pl.BlockSpec((tm, w), lambda i: (i, 0))
    return pl.pallas_call(
        _mix_out_kernel,
        grid=(t // tm,),
        in_specs=[row(D_MODEL), row(MLA_HEADS * 128), row(512), row(D_MODEL), row(D_MODEL),
                  _mod_spec(r, tiles_per_group, 2), _mod_spec(r, tiles_per_group, 3), _mod_spec(r, tiles_per_group, 4),
                  pl.BlockSpec((MLA_HEADS * 128, D_MODEL), const),
                  pl.BlockSpec((512, D_MODEL), const),
                  pl.BlockSpec((D_MODEL, D_MODEL), const),
                  pl.BlockSpec((1, D_MODEL), const),
                  pl.BlockSpec((N_EXPERTS, D_MODEL), const),
                  pl.BlockSpec((N_EXPERTS, 1), const)],
        out_specs=(row(D_MODEL), row(D_MODEL), row(128)),
        out_shape=(jax.ShapeDtypeStruct((t, D_MODEL), F32), jax.ShapeDtypeStruct((t, D_MODEL), BF16),
                   jax.ShapeDtypeStruct((t, 128), F32)),
        compiler_params=_cp("parallel"),
        name="mix_out",
    )(x, oa, ob, ga, gb, mod3, mod3, mod3, w_ova, w_b, w_o, g_ffn, w_rt, rb)


def _moe_kernel(h_ref, g_ref, wgu_ref, wd_ref, wsgu_ref, wsd_ref, x1_ref, gt2_ref, gfin_ref, o_ref, acc):
    e = pl.program_id(1)
    tm = h_ref.shape[0]
    h = h_ref[...]

    @pl.when(e == 0)
    def _():
        sgu = _dot(h, wsgu_ref[...].astype(BF16))
        hid = _silu(sgu[:, 0:D_SHARED]) * sgu[:, D_SHARED:2 * D_SHARED]
        acc[...] = _dot(hid.astype(BF16), wsd_ref[...].astype(BF16))

    gu = _dot(h, wgu_ref[...].astype(BF16))
    hid = _silu(gu[:, 0:D_EXPERT]) * gu[:, D_EXPERT:2 * D_EXPERT]
    lane = lax.broadcasted_iota(jnp.int32, (tm, 128), 1)
    g = jnp.sum(jnp.where(lane == e, g_ref[...], 0.0), axis=1, keepdims=True)
    acc[...] += _dot((hid * g).astype(BF16), wd_ref[...].astype(BF16))

    @pl.when(e == pl.num_programs(1) - 1)
    def _():
        o_ref[...] = _rms(x1_ref[...] + gt2_ref[...] * acc[...], gfin_ref[...])


def _moe(h2, gates, x1, mod3, tiles_per_group, wts, tm):
    t = h2.shape[0]
    r = mod3.shape[1]
    w_gu, w_d, w_sgu, w_sd, g_fin = wts
    row = lambda w: pl.BlockSpec((tm, w), lambda i, e: (i, 0))
    const = lambda i, e: (0, 0)
    return pl.pallas_call(
        _moe_kernel,
        grid=(t // tm, N_EXPERTS),
        in_specs=[row(D_MODEL), row(128),
                  pl.BlockSpec((None, D_MODEL, 2 * D_EXPERT), lambda i, e: (e, 0, 0)),
                  pl.BlockSpec((None, D_EXPERT, D_MODEL), lambda i, e: (e, 0, 0)),
                  pl.BlockSpec((D_MODEL, 2 * D_SHARED), const),
                  pl.BlockSpec((D_SHARED, D_MODEL), const),
                  row(D_MODEL), _mod_spec(r, tiles_per_group, 5, 2),
                  pl.BlockSpec((1, D_MODEL), const)],
        out_specs=row(D_MODEL),
        out_shape=jax.ShapeDtypeStruct((t, D_MODEL), F32),
        scratch_shapes=[pltpu.VMEM((tm, D_MODEL), F32)],
        compiler_params=_cp("parallel", "arbitrary"),
        name="moe",
    )(h2, gates, w_gu, w_d, w_sgu, w_sd, x1, mod3, g_fin)


def _t5_bucket(n):
    n = jnp.maximum(n, 0)
    max_exact = N_BUCKETS // 2
    nf = jnp.maximum(n, 1).astype(F32)
    large = max_exact + (jnp.log(nf / max_exact) / math.log(MAX_DISTANCE / max_exact)
                         * (N_BUCKETS - max_exact)).astype(jnp.int32)
    large = jnp.minimum(large, N_BUCKETS - 1)
    return jnp.where(n < max_exact, n, large)


def _rope_tables(pos):
    half = MLA_ROPE // 2
    inv = ROPE_THETA ** (-jnp.arange(half, dtype=F32) / half)
    ang = pos.astype(F32)[:, None] * inv[None, :]
    cos, sin = jnp.cos(ang), jnp.sin(ang)
    z = jnp.zeros((pos.shape[0], 128 - 2 * half), F32)
    zh = jnp.zeros_like(cos)
    return (jnp.concatenate([cos, cos, z], axis=1), jnp.concatenate([-sin, zh, z], axis=1),
            jnp.concatenate([zh, sin, z], axis=1))


def _tile(t, pref):
    while t % pref:
        pref //= 2
    return pref


def kernel(x_prompt, x_sample, c_prompt, c_sample, cache_mla, cache_k, cache_v, page_table, w_ada, b_ada, g_mix, w_in, g_q, w_uq, g_kv, w_uk, w_uv, lam_q1, lam_k1, lam_q2, lam_k2, g_subln, w_a, w_b, w_o, rel_bias, g_ffn, w_router, router_bias, w_exp_gu, w_exp_down, w_sh_gu, w_sh_down, g_final):
    batch, seq, _ = x_prompt.shape
    nb, dseq, _ = x_sample.shape
    past = page_table.shape[1] * PAGE
    l = 0
    tp, ts = batch * seq, nb * dseq

    w_in_l = w_in[l]
    w_in_p = jnp.concatenate([w_in_l[:, 0:416], jnp.zeros((D_MODEL, 96), F32), w_in_l[:, 416:]], axis=1).astype(BF16)
    uq = w_uq[l].reshape(Q_LORA, MLA_HEADS, MLA_NOPE + MLA_ROPE)
    w_ql = _bmm(jnp.transpose(uq[:, :, :MLA_NOPE], (1, 0, 2)), jnp.transpose(w_uk[l], (1, 2, 0)))
    w_qr = jnp.transpose(uq[:, :, MLA_NOPE:], (1, 0, 2))
    w_q = jnp.concatenate([w_ql, w_qr, jnp.zeros((MLA_HEADS, Q_LORA, QW - MLA_ROW), F32)], axis=2) * MLA_SCALE
    w_q = jnp.transpose(w_q, (1, 0, 2)).reshape(Q_LORA, MLA_HEADS * QW).astype(BF16)
    w_ova = _bmm(jnp.transpose(w_uv[l], (1, 0, 2)), w_a[l].reshape(MLA_HEADS, 64, D_MODEL))
    w_ova = w_ova.reshape(MLA_HEADS * KV_LORA, D_MODEL).astype(BF16)
    in_wts = (g_mix[l].reshape(1, -1), w_in_p, g_q[l].reshape(1, -1), w_q, g_kv[l].reshape(1, -1))
    out_wts = (w_ova, w_b[l].astype(BF16), w_o[l].astype(BF16), g_ffn[l].reshape(1, -1),
               w_router[l].T, router_bias[l].reshape(-1, 1))
    moe_wts = (w_exp_gu[l], w_exp_down[l], w_sh_gu[l], w_sh_down[l], g_final.reshape(1, -1))
    lam_vecs = tuple(v[l].reshape(1, -1) for v in (lam_q1, lam_k1, lam_q2, lam_k2))

    mod = _ada(jnp.concatenate([c_prompt, c_sample], axis=0), w_ada[l], b_ada[l])
    mod_p = mod[:batch].reshape(batch, 1, 6 * D_MODEL)

    tm = _tile(seq, 512)
    xp = x_prompt.reshape(tp, D_MODEL)
    tabs_p = _rope_tables(jnp.arange(seq, dtype=jnp.int32))
    (q, kcat, ckvt, latt_p, dq, dk_p, dv_p, dkb, dvt, ga, gb) = _mixer_in(
        xp, mod_p, seq // tm, tabs_p, seq // tm, in_wts, tm)
    lat_p = jnp.transpose(latt_p, (0, 2, 1))
    nq = seq // BQ
    kk = jnp.arange(BQ, dtype=jnp.int32)[:, None]
    qq = jnp.arange(BQ, dtype=jnp.int32)[None, :]
    dist = jnp.arange(nq, dtype=jnp.int32)[:, None, None] * BQ + qq[None] - kk[None]
    bias_t = jnp.transpose(rel_bias[_t5_bucket(dist)], (3, 0, 1, 2)).astype(F32)
    oa, ob = _attn_prompt(q, kcat, ckvt, dq, dkb, dvt, bias_t, lam_vecs, g_subln[l].reshape(-1, 1), batch, seq)
    x1, h2, gates = _mix_out(xp, oa, ob, ga, gb, mod_p, seq // tm, out_wts, tm)
    tmm = _tile(seq, 1024)
    y_p = _moe(h2, gates, x1, mod_p, seq // tmm, moe_wts, tmm)

    tms = _tile(ts, 256)
    xs = x_sample.reshape(ts, D_MODEL)
    mod_s = jnp.repeat(mod[batch:], dseq, axis=0).reshape(ts // tms, tms, 6 * D_MODEL)
    pos_s = past + (jnp.arange(ts, dtype=jnp.int32) % dseq)
    tabs_s = _rope_tables(pos_s)
    (q_s, kcat_s, _, latt_s, dq_s, dk_s, dv_s, _, _, ga_s, gb_s) = _mixer_in(
        xs, mod_s, 1, tabs_s, ts // tms, in_wts, tms)
    lat_s = jnp.transpose(latt_s[0], (1, 0))
    tq = jnp.arange(dseq, dtype=jnp.int32)
    rows_t = jnp.tile(tq, 8)
    rows_h = jnp.repeat(jnp.arange(DIFF_HEADS, dtype=jnp.int32), 2 * dseq)
    kn = jnp.arange(PAGE, dtype=jnp.int32)
    bias_new = rel_bias[_t5_bucket(rows_t[:, None] - kn[None, :]), rows_h[:, None]].astype(F32)
    kp = jnp.arange(past, dtype=jnp.int32)
    bias_cache = rel_bias[_t5_bucket(past + rows_t[:, None] - kp[None, :]), rows_h[:, None]].astype(F32)
    npg = 8 if page_table.shape[1] % 8 == 0 else 1
    oa_s, ob_s = _attn_sample(q_s, dq_s, kcat_s, dk_s, dv_s, bias_new, bias_cache, lam_vecs,
                              g_subln[l].reshape(1, -1), jnp.transpose(cache_mla[l], (0, 2, 1)),
                              cache_k[l], cache_v[l], page_table, npg)
    x1_s, h2_s, gates_s = _mix_out(xs, oa_s, ob_s, ga_s, gb_s, mod_s, 1, out_wts, tms)
    mod_s2 = mod_s.reshape(1, ts, 6 * D_MODEL) if ts <= 1024 else mod_s
    tmm_s = ts if ts <= 1024 else tms
    y_s = _moe(h2_s, gates_s, x1_s, mod_s2, 1, moe_wts, tmm_s)

    return (y_p.reshape(batch, seq, D_MODEL), y_s.reshape(nb, dseq, D_MODEL),
            lat_p.reshape(1, batch, seq, MLA_ROW), dk_p.reshape(1, batch, seq, DIFF_HEADS, 2 * DIFF_DH),
            dv_p.reshape(1, batch, seq, DIFF_HEADS, DIFF_DV),
            lat_s.reshape(1, nb, dseq, MLA_ROW), dk_s.reshape(1, nb, dseq, DIFF_HEADS, 2 * DIFF_DH),
            dv_s.reshape(1, nb, dseq, DIFF_HEADS, DIFF_DV))
```

```python
import functools
import math

import jax
import jax.numpy as jnp
from jax import lax
from jax.experimental import pallas as pl
from jax.experimental.pallas import tpu as pltpu

F32 = jnp.float32
BF16 = jnp.bfloat16

D_MODEL = 1024
PAGE = 128
MLA_HEADS = 8
MLA_NOPE = 64
MLA_ROPE = 32
Q_LORA = 256
KV_LORA = 128
MLA_ROW = KV_LORA + MLA_ROPE
MLA_SCALE = (MLA_NOPE + MLA_ROPE) ** -0.5
ROPE_THETA = 10000.0
DIFF_HEADS = 4
DIFF_DH = 64
DIFF_DV = 128
DIFF_SCALE = DIFF_DH ** -0.5
N_BUCKETS = 32
MAX_DISTANCE = 128
N_EXPERTS = 64
N_GROUPS = 8
TOPK_GROUPS = 4
TOP_K = 8
D_EXPERT = 256
D_SHARED = 256
ROUTE_SCALE = 2.5
EPS = 1e-6
NEG = -1e30
LAM_INIT = 0.8 - 0.6 * math.exp(-0.3 * 0)
LOG2E = math.log2(math.e)

QW = 256
W_IN_COLS = 4096
BQ = 256
VMEM_LIMIT = 56 * 1024 * 1024


def _cp(*sem):
    return pltpu.CompilerParams(dimension_semantics=sem, vmem_limit_bytes=VMEM_LIMIT)


def _dot(a, b):
    return jnp.dot(a, b, preferred_element_type=F32)


def _dot_nt(a, b):
    return lax.dot_general(a, b, (((1,), (1,)), ((), ())), preferred_element_type=F32)


def _rms(x, g):
    return x * lax.rsqrt(jnp.mean(x * x, axis=-1, keepdims=True) + EPS) * g


def _silu(x):
    return x * jax.nn.sigmoid(x)


def _lam(lq1_ref, lk1_ref, lq2_ref, lk2_ref):
    a = jnp.sum(lq1_ref[...] * lk1_ref[...], axis=-1, keepdims=True)
    b = jnp.sum(lq2_ref[...] * lk2_ref[...], axis=-1, keepdims=True)
    return jnp.exp(a) - jnp.exp(b) + LAM_INIT


def _bmm_kernel(a_ref, b_ref, o_ref):
    o_ref[...] = jnp.dot(a_ref[...], b_ref[...], preferred_element_type=F32, precision=lax.Precision.HIGHEST)


def _bmm(a, b):
    h, m, k = a.shape
    n = b.shape[2]
    return pl.pallas_call(
        _bmm_kernel,
        grid=(h,),
        in_specs=[pl.BlockSpec((None, m, k), lambda i: (i, 0, 0)), pl.BlockSpec((None, k, n), lambda i: (i, 0, 0))],
        out_specs=pl.BlockSpec((None, m, n), lambda i: (i, 0, 0)),
        out_shape=jax.ShapeDtypeStruct((h, m, n), F32),
        compiler_params=_cp("parallel"),
        name="fold_weights",
    )(a, b)


def _ada_kernel(c_ref, w_ref, b_ref, o_ref):
    c = c_ref[...]
    o_ref[...] = _dot(_silu(c).astype(BF16), w_ref[...].astype(BF16)) + b_ref[...]


def _ada(c, w, b):
    n = c.shape[0]
    return pl.pallas_call(
        _ada_kernel,
        grid=(6,),
        in_specs=[pl.BlockSpec((n, D_MODEL), lambda j: (0, 0)),
                  pl.BlockSpec((D_MODEL, D_MODEL), lambda j: (0, j)),
                  pl.BlockSpec((1, D_MODEL), lambda j: (0, j))],
        out_specs=pl.BlockSpec((n, D_MODEL), lambda j: (0, j)),
        out_shape=jax.ShapeDtypeStruct((n, 6 * D_MODEL), F32),
        compiler_params=_cp("parallel"),
        name="adaln",
    )(c, w, b.reshape(1, -1))


def _mod_spec(r, tiles_per_group, k, grid_rank=1):
    if grid_rank == 1:
        return pl.BlockSpec((None, r, D_MODEL), lambda i: (i // tiles_per_group, 0, k))
    return pl.BlockSpec((None, r, D_MODEL), lambda i, e: (i // tiles_per_group, 0, k))


def _rope128(v, c, s1, s2):
    return v * c + pltpu.roll(v, 112, 1) * s1 + pltpu.roll(v, 16, 1) * s2


def _in_kernel(x_ref, sh1_ref, sc1_ref, gmix_ref, win_ref, gq_ref, wq_ref, gkv_ref, c_ref, s1_ref, s2_ref,
               q_ref, kcat_ref, ckvt_ref, latt_ref, dq_ref, dk_ref, dv_ref, dkb_ref, dvt_ref, ga_ref, gb_ref):
    tm = x_ref.shape[0]
    x = x_ref[...]
    h = _rms(x, gmix_ref[...]) * (1.0 + sc1_ref[...]) + sh1_ref[...]
    hb = h.astype(BF16)
    c, s1, s2 = c_ref[...], s1_ref[...], s2_ref[...]

    qn = _rms(_dot(hb, win_ref[:, 0:256]), gq_ref[...]).astype(BF16)
    for hd in range(MLA_HEADS):
        qh = _dot(qn, wq_ref[:, hd * QW:(hd + 1) * QW])
        q_ref[:, hd * QW:hd * QW + 128] = qh[:, 0:128].astype(BF16)
        q_ref[:, hd * QW + 128:(hd + 1) * QW] = _rope128(qh[:, 128:256], c, s1, s2).astype(BF16)

    ckv = _rms(_dot(hb, win_ref[:, 256:384]), gkv_ref[...])
    kr = _rope128(_dot(hb, win_ref[:, 384:512]), c, s1, s2)
    ckv_t = ckv.T
    latt_ref[0:KV_LORA, :] = ckv_t
    latt_ref[KV_LORA:MLA_ROW, :] = kr.T[0:MLA_ROPE, :]
    kcat_ref[:, 0:128] = ckv.astype(BF16)
    kcat_ref[:, 128:256] = kr.astype(BF16)
    for cc in range(tm // BQ):
        ckvt_ref[cc] = ckv_t[:, cc * BQ:(cc + 1) * BQ].astype(BF16)

    dq_ref[...] = (_dot(hb, win_ref[:, 512:1024]) * (DIFF_SCALE * LOG2E)).astype(BF16)
    dk = _dot(hb, win_ref[:, 1024:1536])
    dkb_ref[...] = dk.astype(BF16)
    dv = _dot(hb, win_ref[:, 1536:2048])
    for hd in range(DIFF_HEADS):
        rows = pl.ds(hd, tm, stride=DIFF_HEADS)
        dk_ref[rows, :] = dk[:, hd * 128:(hd + 1) * 128]
        dv_ref[rows, :] = dv[:, hd * 128:(hd + 1) * 128]
    for cc in range(tm // BQ):
        for hd in range(DIFF_HEADS):
            dvt_ref[cc, hd] = dv[cc * BQ:(cc + 1) * BQ, hd * 128:(hd + 1) * 128].T.astype(BF16)
    ga_ref[...] = jax.nn.sigmoid(_dot(hb, win_ref[:, 2048:3072])).astype(BF16)
    gb_ref[...] = jax.nn.sigmoid(_dot(hb, win_ref[:, 3072:4096])).astype(BF16)


def _mixer_in(x, mod3, tiles_per_group, tabs, tab_blocks, wts, tm):
    t = x.shape[0]
    r = mod3.shape[1]
    nt = t // tm
    c_t, s1_t, s2_t = tabs
    g_mix, w_in_p, g_q, w_q, g_kv = wts
    const = lambda i: (0, 0)
    tab_spec = pl.BlockSpec((tm, 128), lambda i: (i % tab_blocks, 0))
    row = lambda w: pl.BlockSpec((tm, w), lambda i: (i, 0))
    nck = tm // BQ
    out_shape = (
        jax.ShapeDtypeStruct((t, MLA_HEADS * QW), BF16),
        jax.ShapeDtypeStruct((t, QW), BF16),
        jax.ShapeDtypeStruct((t // BQ, 128, BQ), BF16),
        jax.ShapeDtypeStruct((t // (tab_blocks * tm), MLA_ROW, tab_blocks * tm), F32),
        jax.ShapeDtypeStruct((t, 512), BF16),
        jax.ShapeDtypeStruct((t * DIFF_HEADS, 128), F32),
        jax.ShapeDtypeStruct((t * DIFF_HEADS, 128), F32),
        jax.ShapeDtypeStruct((t, 512), BF16),
        jax.ShapeDtypeStruct((t // BQ, DIFF_HEADS, 128, BQ), BF16),
        jax.ShapeDtypeStruct((t, D_MODEL), BF16),
        jax.ShapeDtypeStruct((t, D_MODEL), BF16),
    )
    out_specs = (
        row(MLA_HEADS * QW), row(QW),
        pl.BlockSpec((nck, 128, BQ), lambda i: (i, 0, 0)),
        pl.BlockSpec((None, MLA_ROW, tm), lambda i: (i // tab_blocks, 0, i % tab_blocks)),
        row(512),
        pl.BlockSpec((tm * DIFF_HEADS, 128), lambda i: (i, 0)),
        pl.BlockSpec((tm * DIFF_HEADS, 128), lambda i: (i, 0)),
        row(512),
        pl.BlockSpec((nck, DIFF_HEADS, 128, BQ), lambda i: (i, 0, 0, 0)),
        row(D_MODEL), row(D_MODEL),
    )
    return pl.pallas_call(
        _in_kernel,
        grid=(nt,),
        in_specs=[row(D_MODEL), _mod_spec(r, tiles_per_group, 0), _mod_spec(r, tiles_per_group, 1),
                  pl.BlockSpec((1, D_MODEL), const),
                  pl.BlockSpec((D_MODEL, W_IN_COLS), const),
                  pl.BlockSpec((1, Q_LORA), const),
                  pl.BlockSpec((Q_LORA, MLA_HEADS * QW), const),
                  pl.BlockSpec((1, KV_LORA), const),
                  tab_spec, tab_spec, tab_spec],
        out_specs=out_specs,
        out_shape=out_shape,
        compiler_params=_cp("parallel"),
        name="mixer_in",
    )(x, mod3, mod3, g_mix, w_in_p, g_q, w_q, g_kv, c_t, s1_t, s2_t)


def _attn_prompt_kernel(q_ref, kc_ref, ckvt_ref, dq_ref, dk_ref, dvt_ref, bias_ref,
                        lq1_ref, lk1_ref, lq2_ref, lk2_ref, gsub_ref,
                        oa_ref, ob_ref, m_a, l_a, acc_a, m_d, l_d, acc_d, qbd):
    i = pl.program_id(1)
    m_a[...] = jnp.full(m_a.shape, NEG, F32)
    l_a[...] = jnp.zeros(l_a.shape, F32)
    acc_a[...] = jnp.zeros(acc_a.shape, F32)
    m_d[...] = jnp.full(m_d.shape, NEG, F32)
    l_d[...] = jnp.zeros(l_d.shape, F32)
    acc_d[...] = jnp.zeros(acc_d.shape, F32)
    lane = lax.broadcasted_iota(jnp.int32, (BQ, 128), 1)
    for hd in range(DIFF_HEADS):
        dqh = dq_ref[:, hd * 128:(hd + 1) * 128].astype(F32)
        qbd[hd, 0:BQ, :] = jnp.where(lane < DIFF_DH, dqh, 0.0).astype(BF16)
        qbd[hd, BQ:2 * BQ, :] = jnp.where(lane >= DIFF_DH, dqh, 0.0).astype(BF16)
    krow = lax.broadcasted_iota(jnp.int32, (BQ, BQ), 0)
    qcol = lax.broadcasted_iota(jnp.int32, (BQ, BQ), 1)

    def step(j, mask):
        k0 = pl.multiple_of(j * BQ, BQ)
        kc = kc_ref[pl.ds(k0, BQ), :]
        vt = ckvt_ref[j]
        for hd in range(MLA_HEADS):
            s = _dot_nt(kc, q_ref[:, hd * QW:(hd + 1) * QW])
            if mask is not None:
                s = jnp.where(mask, s, NEG)
            m_prev = m_a[hd:hd + 1, :]
            m_new = jnp.maximum(m_prev, jnp.max(s, axis=0, keepdims=True))
            alpha = jnp.exp2(m_prev - m_new)
            p = jnp.exp2(s - m_new)
            l_a[hd:hd + 1, :] = alpha * l_a[hd:hd + 1, :] + jnp.sum(p, axis=0, keepdims=True)
            m_a[hd:hd + 1, :] = m_new
            acc_a[hd] = alpha * acc_a[hd] + _dot(vt, p.astype(BF16))
        mask2 = None if mask is None else jnp.concatenate([mask, mask], axis=1)
        for hd in range(DIFF_HEADS):
            dkc = dk_ref[pl.ds(k0, BQ), hd * 128:(hd + 1) * 128]
            b = bias_ref[hd, i - j]
            s = _dot_nt(dkc, qbd[hd]) + jnp.concatenate([b, b], axis=1)
            if mask2 is not None:
                s = jnp.where(mask2, s, NEG)
            m_prev = m_d[hd:hd + 1, :]
            m_new = jnp.maximum(m_prev, jnp.max(s, axis=0, keepdims=True))
            alpha = jnp.exp2(m_prev - m_new)
            p = jnp.exp2(s - m_new)
            l_d[hd:hd + 1, :] = alpha * l_d[hd:hd + 1, :] + jnp.sum(p, axis=0, keepdims=True)
            m_d[hd:hd + 1, :] = m_new
            acc_d[hd] = alpha * acc_d[hd] + _dot(dvt_ref[j, hd], p.astype(BF16))

    def body(j, carry):
        step(j, None)
        return carry

    lax.fori_loop(0, i, body, 0)
    step(i, krow <= qcol)

    for hd in range(MLA_HEADS):
        ot = acc_a[hd] / l_a[hd:hd + 1, :]
        oa_ref[:, hd * 128:(hd + 1) * 128] = ot.T.astype(BF16)
    lam = _lam(lq1_ref, lk1_ref, lq2_ref, lk2_ref)
    for hd in range(DIFF_HEADS):
        a = acc_d[hd]
        l = l_d[hd:hd + 1, :]
        ot = a[:, 0:BQ] / l[:, 0:BQ] - lam * (a[:, BQ:2 * BQ] / l[:, BQ:2 * BQ])
        ms = jnp.mean(ot * ot, axis=0, keepdims=True)
        ot = ot * lax.rsqrt(ms + EPS) * gsub_ref[...] * (1.0 - LAM_INIT)
        ob_ref[:, hd * 128:(hd + 1) * 128] = ot.T.astype(BF16)


def _attn_prompt(q, kcat, ckvt, dq, dkb, dvt, bias_t, lam_vecs, gsub_col, batch, seq):
    t = q.shape[0]
    nq = seq // BQ
    qrow = lambda w: pl.BlockSpec((BQ, w), lambda b, i: (b * nq + i, 0))
    vec = pl.BlockSpec((1, DIFF_DH), lambda b, i: (0, 0))
    return pl.pallas_call(
        _attn_prompt_kernel,
        grid=(batch, nq),
        in_specs=[qrow(MLA_HEADS * QW),
                  pl.BlockSpec((seq, QW), lambda b, i: (b, 0)),
                  pl.BlockSpec((nq, 128, BQ), lambda b, i: (b, 0, 0)),
                  qrow(512),
                  pl.BlockSpec((seq, 512), lambda b, i: (b, 0)),
                  pl.BlockSpec((nq, DIFF_HEADS, 128, BQ), lambda b, i: (b, 0, 0, 0)),
                  pl.BlockSpec((DIFF_HEADS, nq, BQ, BQ), lambda b, i: (0, 0, 0, 0)),
                  vec, vec, vec, vec,
                  pl.BlockSpec((DIFF_DV, 1), lambda b, i: (0, 0))],
        out_specs=(qrow(MLA_HEADS * 128), qrow(512)),
        out_shape=(jax.ShapeDtypeStruct((t, MLA_HEADS * 128), BF16), jax.ShapeDtypeStruct((t, 512), BF16)),
        scratch_shapes=[pltpu.VMEM((MLA_HEADS, BQ), F32), pltpu.VMEM((MLA_HEADS, BQ), F32),
                        pltpu.VMEM((MLA_HEADS, 128, BQ), F32),
                        pltpu.VMEM((DIFF_HEADS, 2 * BQ), F32), pltpu.VMEM((DIFF_HEADS, 2 * BQ), F32),
                        pltpu.VMEM((DIFF_HEADS, 128, 2 * BQ), F32),
                        pltpu.VMEM((DIFF_HEADS, 2 * BQ, 128), BF16)],
        compiler_params=_cp("parallel", "arbitrary"),
        name="attn_prompt",
    )(q, kcat, ckvt, dq, dkb, dvt, bias_t, *lam_vecs, gsub_col)


def _attn_sample_kernel(npg, pt_ref, q_ref, dq_ref, kcn_ref, dkn_ref, dvn_ref, biasn_ref, bias_ref,
                        lq1_ref, lk1_ref, lq2_ref, lk2_ref, gsub_ref, *rest):
    lat_pages = rest[0:npg]
    k_pages = rest[npg:2 * npg]
    v_pages = rest[2 * npg:3 * npg]
    oa_ref, ob_ref = rest[3 * npg:3 * npg + 2]
    (lattb, kb, vb, pgl, pgk, pgv, qs, qbd, m_a, l_a, acc_a, m_d, l_d, acc_d) = rest[3 * npg + 2:]
    del pt_ref
    j = pl.program_id(1)
    nrow = MLA_HEADS * 8

    def update(nk, bias, mask):
        s = _dot(qs[...].astype(BF16), lattb[:, 0:nk])
        if mask is not None:
            s = jnp.where(mask, s, NEG)
        m_prev = m_a[...]
        m_new = jnp.maximum(m_prev, jnp.max(s, axis=1, keepdims=True))
        alpha = jnp.exp2(m_prev - m_new)
        p = jnp.exp2(s - m_new[:, 0:1])
        l_a[...] = alpha * l_a[...] + jnp.sum(p, axis=1, keepdims=True)
        m_a[...] = m_new
        acc_a[...] = alpha * acc_a[...] + _dot_nt(p.astype(BF16), lattb[0:KV_LORA, 0:nk])
        s = _dot_nt(qbd[...].astype(BF16), kb[0:nk, :]) + bias
        if mask is not None:
            s = jnp.where(mask, s, NEG)
        m_prev = m_d[...]
        m_new = jnp.maximum(m_prev, jnp.max(s, axis=1, keepdims=True))
        alpha = jnp.exp2(m_prev - m_new)
        p = jnp.exp2(s - m_new[:, 0:1])
        l_d[...] = alpha * l_d[...] + jnp.sum(p, axis=1, keepdims=True)
        m_d[...] = m_new
        acc_d[...] = alpha[:, 0:1] * acc_d[...] + _dot(p.astype(BF16), vb[0:nk, :])

    @pl.when(j == 0)
    def _():
        m_a[...] = jnp.full(m_a.shape, NEG, F32)
        l_a[...] = jnp.zeros(l_a.shape, F32)
        acc_a[...] = jnp.zeros(acc_a.shape, F32)
        m_d[...] = jnp.full(m_d.shape, NEG, F32)
        l_d[...] = jnp.zeros(l_d.shape, F32)
        acc_d[...] = jnp.zeros(acc_d.shape, F32)
        lane = lax.broadcasted_iota(jnp.int32, (8, 512), 1)
        dq = dq_ref[...].astype(F32)
        for hd in range(MLA_HEADS):
            qs[hd * 8:(hd + 1) * 8, :] = q_ref[:, hd * QW:(hd + 1) * QW].astype(F32)
        for hd in range(DIFF_HEADS):
            for mp in range(2):
                lo = hd * 128 + mp * DIFF_DH
                r0 = (hd * 2 + mp) * 8
                qbd[r0:r0 + 8, :] = jnp.where((lane >= lo) & (lane < lo + DIFF_DH), dq, 0.0)
        pgl[...] = jnp.zeros(pgl.shape, F32)
        pgk[...] = jnp.zeros(pgk.shape, F32)
        pgv[...] = jnp.zeros(pgv.shape, F32)
        pgl[0:8, :] = kcn_ref[...].astype(F32)
        pgk[0:8, :] = dkn_ref[...]
        pgv[0:8, :] = dvn_ref[...]
        lattb[:, 0:PAGE] = pgl[...].T.astype(BF16)
        kb[0:PAGE, :] = pgk[...].astype(BF16)
        vb[0:PAGE, :] = pgv[...].astype(BF16)
        row = lax.broadcasted_iota(jnp.int32, (nrow, PAGE), 0)
        col = lax.broadcasted_iota(jnp.int32, (nrow, PAGE), 1)
        update(PAGE, biasn_ref[...], col <= (row % 8))
        lattb[...] = jnp.zeros(lattb.shape, BF16)

    for u in range(npg):
        lattb[0:MLA_ROW, u * PAGE:(u + 1) * PAGE] = lat_pages[u][...].astype(BF16)
        for hd in range(DIFF_HEADS):
            rows = pl.ds(hd, PAGE, stride=DIFF_HEADS)
            kb[u * PAGE:(u + 1) * PAGE, hd * 128:(hd + 1) * 128] = k_pages[u][rows, :].astype(BF16)
            vb[u * PAGE:(u + 1) * PAGE, hd * 128:(hd + 1) * 128] = v_pages[u][rows, :].astype(BF16)
    update(npg * PAGE, bias_ref[...], None)

    @pl.when(j == pl.num_programs(1) - 1)
    def _():
        oa = acc_a[...] / l_a[...]
        for hd in range(MLA_HEADS):
            oa_ref[:, hd * 128:(hd + 1) * 128] = oa[hd * 8:(hd + 1) * 8, :].astype(BF16)
        lam = _lam(lq1_ref, lk1_ref, lq2_ref, lk2_ref)
        od = acc_d[...] / l_d[:, 0:1]
        for hd in range(DIFF_HEADS):
            o1 = od[hd * 16:hd * 16 + 8, hd * 128:(hd + 1) * 128]
            o2 = od[hd * 16 + 8:hd * 16 + 16, hd * 128:(hd + 1) * 128]
            o = _rms(o1 - lam * o2, gsub_ref[...]) * (1.0 - LAM_INIT)
            ob_ref[:, hd * 128:(hd + 1) * 128] = o.astype(BF16)


def _attn_sample(q, dq, kcat, dk, dv, bias_new, bias_cache, lam_vecs, gsub_row, cache_mla_t, cache_k, cache_v,
                 page_table, npg):
    nb = page_table.shape[0]
    nsteps = page_table.shape[1] // npg
    nk = npg * PAGE
    q3 = q.reshape(nb, 8, MLA_HEADS * QW)
    dq3 = dq.reshape(nb, 8, 512)
    kc3 = kcat.reshape(nb, 8, QW)
    dk3 = dk.reshape(nb, 8, 512)
    dv3 = dv.reshape(nb, 8, 512)
    prow = PAGE * DIFF_HEADS
    ck = cache_k.reshape(-1, 128)
    cv = cache_v.reshape(-1, 128)
    seqblk = lambda w: pl.BlockSpec((None, 8, w), lambda b, j, pt: (b, 0, 0))
    vec = pl.BlockSpec((1, DIFF_DH), lambda b, j, pt: (0, 0))

    def lat_spec(u):
        return pl.BlockSpec((None, MLA_ROW, PAGE), lambda b, j, pt: (pt[b, j * npg + u], 0, 0))

    def kv_spec(u):
        return pl.BlockSpec((prow, 128), lambda b, j, pt: (pt[b, j * npg + u], 0))

    in_specs = [seqblk(MLA_HEADS * QW), seqblk(512), seqblk(QW), seqblk(512), seqblk(512),
                pl.BlockSpec((64, PAGE), lambda b, j, pt: (0, 0)),
                pl.BlockSpec((64, nk), lambda b, j, pt: (0, j)),
                vec, vec, vec, vec,
                pl.BlockSpec((1, DIFF_DV), lambda b, j, pt: (0, 0))]
    in_specs += [lat_spec(u) for u in range(npg)]
    in_specs += [kv_spec(u) for u in range(npg)]
    in_specs += [kv_spec(u) for u in range(npg)]
    grid_spec = pltpu.PrefetchScalarGridSpec(
        num_scalar_prefetch=1,
        grid=(nb, nsteps),
        in_specs=in_specs,
        out_specs=(seqblk(MLA_HEADS * 128), seqblk(512)),
        scratch_shapes=[pltpu.VMEM((QW, nk), BF16), pltpu.VMEM((nk, 512), BF16), pltpu.VMEM((nk, 512), BF16),
                        pltpu.VMEM((PAGE, QW), F32), pltpu.VMEM((PAGE, 512), F32), pltpu.VMEM((PAGE, 512), F32),
                        pltpu.VMEM((64, QW), F32), pltpu.VMEM((64, 512), F32),
                        pltpu.VMEM((64, 128), F32), pltpu.VMEM((64, 128), F32), pltpu.VMEM((64, 128), F32),
                        pltpu.VMEM((64, 128), F32), pltpu.VMEM((64, 128), F32), pltpu.VMEM((64, 512), F32)],
    )
    oa, ob = pl.pallas_call(
        functools.partial(_attn_sample_kernel, npg),
        grid_spec=grid_spec,
        out_shape=(jax.ShapeDtypeStruct((nb, 8, MLA_HEADS * 128), BF16), jax.ShapeDtypeStruct((nb, 8, 512), BF16)),
        compiler_params=_cp("parallel", "arbitrary"),
        name="attn_sample",
    )(page_table, q3, dq3, kc3, dk3, dv3, bias_new, bias_cache, *lam_vecs, gsub_row,
      *([cache_mla_t] * npg), *([ck] * npg), *([cv] * npg))
    return oa.reshape(nb * 8, MLA_HEADS * 128), ob.reshape(nb * 8, 512)


def _first_max(v, idx, big):
    m = jnp.max(v, axis=0, keepdims=True)
    f = jnp.min(jnp.where(v == m, idx, big), axis=0, keepdims=True)
    return m, idx == f


def _mix_out_kernel(x_ref, oa_ref, ob_ref, ga_ref, gb_ref, gt1_ref, sh2_ref, sc2_ref, wova_ref, wb_ref, wo_ref, gffn_ref,
                    wrt_ref, rb_ref, x1_ref, h2_ref, gates_ref):
    tm = x_ref.shape[0]
    a = (ga_ref[...].astype(F32) * _dot(oa_ref[...], wova_ref[...])
         + gb_ref[...].astype(F32) * _dot(ob_ref[...], wb_ref[...]))
    x1 = x_ref[...] + gt1_ref[...] * _dot(a.astype(BF16), wo_ref[...])
    x1_ref[...] = x1
    h2 = _rms(x1, gffn_ref[...]) * (1.0 + sc2_ref[...]) + sh2_ref[...]
    h2_ref[...] = h2.astype(BF16)

    logits = lax.dot_general(wrt_ref[...], h2, (((1,), (1,)), ((), ())), preferred_element_type=F32,
                             precision=lax.Precision.HIGHEST)
    aff = jax.nn.sigmoid(logits)
    sel = aff + rb_ref[...]
    gsz = N_EXPERTS // N_GROUPS
    idx8 = lax.broadcasted_iota(jnp.int32, (gsz, tm), 0).astype(F32)
    scores = []
    for g in range(N_GROUPS):
        v = sel[g * gsz:(g + 1) * gsz, :]
        m1, hit = _first_max(v, idx8, float(gsz))
        m2 = jnp.max(jnp.where(hit, -jnp.inf, v), axis=0, keepdims=True)
        scores.append(m1 + m2)
    gs = jnp.concatenate(scores, axis=0)
    gidx = lax.broadcasted_iota(jnp.int32, (N_GROUPS, tm), 0).astype(F32)
    gsel = jnp.zeros((N_GROUPS, tm), F32)
    for _ in range(TOPK_GROUPS):
        _, hit = _first_max(gs, gidx, float(N_GROUPS))
        gsel = jnp.where(hit, 1.0, gsel)
        gs = jnp.where(hit, -jnp.inf, gs)
    cand = jnp.concatenate(
        [jnp.where(gsel[g:g + 1, :] > 0.0, sel[g * gsz:(g + 1) * gsz, :], -jnp.inf) for g in range(N_GROUPS)], axis=0)
    eidx = lax.broadcasted_iota(jnp.int32, (N_EXPERTS, tm), 0).astype(F32)
    w = jnp.zeros((N_EXPERTS, tm), F32)
    for _ in range(TOP_K):
        _, hit = _first_max(cand, eidx, float(N_EXPERTS))
        w = jnp.where(hit, aff, w)
        cand = jnp.where(hit, -jnp.inf, cand)
    gates_t = w / jnp.sum(w, axis=0, keepdims=True) * ROUTE_SCALE
    gates_ref[...] = jnp.concatenate([gates_t, jnp.zeros((128 - N_EXPERTS, tm), F32)], axis=0).T


def _mix_out(x, oa, ob, ga, gb, mod3, tiles_per_group, wts, tm):
    t = x.shape[0]
    r = mod3.shape[1]
    w_ova, w_b, w_o, g_ffn, w_rt, rb = wts
    const = lambda i: (0, 0)
    row = lambda w: pl.BlockSpec((tm, w), lambda i: (i, 0))
    return pl.pallas_call(
        _mix_out_kernel,
        grid=(t // tm,),
        in_specs=[row(D_MODEL), row(MLA_HEADS * 128), row(512), row(D_MODEL), row(D_MODEL),
                  _mod_spec(r, tiles_per_group, 2), _mod_spec(r, tiles_per_group, 3), _mod_spec(r, tiles_per_group, 4),
                  pl.BlockSpec((MLA_HEADS * 128, D_MODEL), const),
                  pl.BlockSpec((512, D_MODEL), const),
                  pl.BlockSpec((D_MODEL, D_MODEL), const),
                  pl.BlockSpec((1, D_MODEL), const),
                  pl.BlockSpec((N_EXPERTS, D_MODEL), const),
                  pl.BlockSpec((N_EXPERTS, 1), const)],
        out_specs=(row(D_MODEL), row(D_MODEL), row(128)),
        out_shape=(jax.ShapeDtypeStruct((t, D_MODEL), F32), jax.ShapeDtypeStruct((t, D_MODEL), BF16),
                   jax.ShapeDtypeStruct((t, 128), F32)),
        compiler_params=_cp("parallel"),
        name="mix_out",
    )(x, oa, ob, ga, gb, mod3, mod3, mod3, w_ova, w_b, w_o, g_ffn, w_rt, rb)


def _moe_kernel(h_ref, g_ref, wgu_ref, wd_ref, wsgu_ref, wsd_ref, x1_ref, gt2_ref, gfin_ref, o_ref, acc):
    e = pl.program_id(1)
    tm = h_ref.shape[0]
    h = h_ref[...]

    @pl.when(e == 0)
    def _():
        sgu = _dot(h, wsgu_ref[...].astype(BF16))
        hid = _silu(sgu[:, 0:D_SHARED]) * sgu[:, D_SHARED:2 * D_SHARED]
        acc[...] = _dot(hid.astype(BF16), wsd_ref[...].astype(BF16))

    gu = _dot(h, wgu_ref[...].astype(BF16))
    hid = _silu(gu[:, 0:D_EXPERT]) * gu[:, D_EXPERT:2 * D_EXPERT]
    lane = lax.broadcasted_iota(jnp.int32, (tm, 128), 1)
    g = jnp.sum(jnp.where(lane == e, g_ref[...], 0.0), axis=1, keepdims=True)
    acc[...] += _dot((hid * g).astype(BF16), wd_ref[...].astype(BF16))

    @pl.when(e == pl.num_programs(1) - 1)
    def _():
        o_ref[...] = _rms(x1_ref[...] + gt2_ref[...] * acc[...], gfin_ref[...])


def _moe(h2, gates, x1, mod3, tiles_per_group, wts, tm):
    t = h2.shape[0]
    r = mod3.shape[1]
    w_gu, w_d, w_sgu, w_sd, g_fin = wts
    row = lambda w: pl.BlockSpec((tm, w), lambda i, e: (i, 0))
    const = lambda i, e: (0, 0)
    return pl.pallas_call(
        _moe_kernel,
        grid=(t // tm, N_EXPERTS),
        in_specs=[row(D_MODEL), row(128),
                  pl.BlockSpec((None, D_MODEL, 2 * D_EXPERT), lambda i, e: (e, 0, 0)),
                  pl.BlockSpec((None, D_EXPERT, D_MODEL), lambda i, e: (e, 0, 0)),
                  pl.BlockSpec((D_MODEL, 2 * D_SHARED), const),
                  pl.BlockSpec((D_SHARED, D_MODEL), const),
                  row(D_MODEL), _mod_spec(r, tiles_per_group, 5, 2),
                  pl.BlockSpec((1, D_MODEL), const)],
        out_specs=row(D_MODEL),
        out_shape=jax.ShapeDtypeStruct((t, D_MODEL), F32),
        scratch_shapes=[pltpu.VMEM((tm, D_MODEL), F32)],
        compiler_params=_cp("parallel", "arbitrary"),
        name="moe",
    )(h2, gates, w_gu, w_d, w_sgu, w_sd, x1, mod3, g_fin)


def _t5_bucket(n):
    n = jnp.maximum(n, 0)
    max_exact = N_BUCKETS // 2
    nf = jnp.maximum(n, 1).astype(F32)
    large = max_exact + (jnp.log(nf / max_exact) / math.log(MAX_DISTANCE / max_exact)
                         * (N_BUCKETS - max_exact)).astype(jnp.int32)
    large = jnp.minimum(large, N_BUCKETS - 1)
    return jnp.where(n < max_exact, n, large)


def _t5_by_distance(rel_bias, nmax):
    hit = _t5_bucket(jnp.arange(nmax, dtype=jnp.int32))[:, None] == jnp.arange(N_BUCKETS, dtype=jnp.int32)[None, :]
    f = jnp.sum(jnp.where(hit[:, :, None], rel_bias[None].astype(F32), 0.0), axis=1)
    return f.T * LOG2E


def _t5_prompt_tiles(f, nq):
    h, c = f.shape[0], nq * BQ
    ln = c + BQ
    v = jnp.concatenate([f[:, :c], jnp.broadcast_to(f[:, 0:1], (h, BQ))], axis=1)
    m = jnp.tile(v, (1, BQ))[:, :BQ * (ln - 1)].reshape(h, BQ, ln - 1)[:, :, :c]
    return jnp.transpose(m.reshape(h, BQ, nq, BQ), (0, 2, 1, 3))


def _t5_sample_rows(f, past, dseq):
    rf = jnp.flip(f[:, :past + dseq], axis=1)
    per_t = jnp.stack([rf[:, dseq - 1 - t:dseq - 1 - t + past] for t in range(dseq)], axis=1)
    h = f.shape[0]
    return jnp.broadcast_to(per_t[:, None], (h, 2, dseq, past)).reshape(h * 2 * dseq, past)


def _rope_tables(pos):
    half = MLA_ROPE // 2
    inv = ROPE_THETA ** (-jnp.arange(half, dtype=F32) / half)
    ang = pos.astype(F32)[:, None] * inv[None, :]
    cos, sin = jnp.cos(ang), jnp.sin(ang)
    z = jnp.zeros((pos.shape[0], 128 - 2 * half), F32)
    zh = jnp.zeros_like(cos)
    return (jnp.concatenate([cos, cos, z], axis=1), jnp.concatenate([-sin, zh, z], axis=1),
            jnp.concatenate([zh, sin, z], axis=1))


def _tile(t, pref):
    while t % pref:
        pref //= 2
    return pref


def kernel(x_prompt, x_sample, c_prompt, c_sample, cache_mla, cache_k, cache_v, page_table, w_ada, b_ada, g_mix, w_in, g_q, w_uq, g_kv, w_uk, w_uv, lam_q1, lam_k1, lam_q2, lam_k2, g_subln, w_a, w_b, w_o, rel_bias, g_ffn, w_router, router_bias, w_exp_gu, w_exp_down, w_sh_gu, w_sh_down, g_final):
    batch, seq, _ = x_prompt.shape
    nb, dseq, _ = x_sample.shape
    past = page_table.shape[1] * PAGE
    l = 0
    tp, ts = batch * seq, nb * dseq

    w_in_l = w_in[l]
    w_in_p = jnp.concatenate([w_in_l[:, 0:416], jnp.zeros((D_MODEL, 96), F32), w_in_l[:, 416:]], axis=1).astype(BF16)
    uq = w_uq[l].reshape(Q_LORA, MLA_HEADS, MLA_NOPE + MLA_ROPE)
    w_ql = _bmm(jnp.transpose(uq[:, :, :MLA_NOPE], (1, 0, 2)), jnp.transpose(w_uk[l], (1, 2, 0)))
    w_qr = jnp.transpose(uq[:, :, MLA_NOPE:], (1, 0, 2))
    w_q = jnp.concatenate([w_ql, w_qr, jnp.zeros((MLA_HEADS, Q_LORA, QW - MLA_ROW), F32)], axis=2)
    w_q = w_q * (MLA_SCALE * LOG2E)
    w_q = jnp.transpose(w_q, (1, 0, 2)).reshape(Q_LORA, MLA_HEADS * QW).astype(BF16)
    w_ova = _bmm(jnp.transpose(w_uv[l], (1, 0, 2)), w_a[l].reshape(MLA_HEADS, 64, D_MODEL))
    w_ova = w_ova.reshape(MLA_HEADS * KV_LORA, D_MODEL).astype(BF16)
    in_wts = (g_mix[l].reshape(1, -1), w_in_p, g_q[l].reshape(1, -1), w_q, g_kv[l].reshape(1, -1))
    out_wts = (w_ova, w_b[l].astype(BF16), w_o[l].astype(BF16), g_ffn[l].reshape(1, -1),
               w_router[l].T, router_bias[l].reshape(-1, 1))
    moe_wts = (w_exp_gu[l], w_exp_down[l], w_sh_gu[l], w_sh_down[l], g_final.reshape(1, -1))
    lam_vecs = tuple(v[l].reshape(1, -1) for v in (lam_q1, lam_k1, lam_q2, lam_k2))

    mod = _ada(jnp.concatenate([c_prompt, c_sample], axis=0), w_ada[l], b_ada[l])
    mod_p = mod[:batch].reshape(batch, 1, 6 * D_MODEL)

    tm = _tile(seq, 512)
    xp = x_prompt.reshape(tp, D_MODEL)
    tabs_p = _rope_tables(jnp.arange(seq, dtype=jnp.int32))
    (q, kcat, ckvt, latt_p, dq, dk_p, dv_p, dkb, dvt, ga, gb) = _mixer_in(
        xp, mod_p, seq // tm, tabs_p, seq // tm, in_wts, tm)
    lat_p = jnp.transpose(latt_p, (0, 2, 1))
    nq = seq // BQ
    f_dist = _t5_by_distance(rel_bias, max(seq, past + dseq))
    bias_t = _t5_prompt_tiles(f_dist, nq)
    oa, ob = _attn_prompt(q, kcat, ckvt, dq, dkb, dvt, bias_t, lam_vecs, g_subln[l].reshape(-1, 1), batch, seq)
    x1, h2, gates = _mix_out(xp, oa, ob, ga, gb, mod_p, seq // tm, out_wts, tm)
    tmm = _tile(seq, 1024)
    y_p = _moe(h2, gates, x1, mod_p, seq // tmm, moe_wts, tmm)

    tms = _tile(ts, 256)
    xs = x_sample.reshape(ts, D_MODEL)
    mod_s = jnp.repeat(mod[batch:], dseq, axis=0).reshape(ts // tms, tms, 6 * D_MODEL)
    pos_s = past + (jnp.arange(ts, dtype=jnp.int32) % dseq)
    tabs_s = _rope_tables(pos_s)
    (q_s, kcat_s, _, latt_s, dq_s, dk_s, dv_s, _, _, ga_s, gb_s) = _mixer_in(
        xs, mod_s, 1, tabs_s, ts // tms, in_wts, tms)
    lat_s = jnp.transpose(latt_s[0], (1, 0))
    fz = jnp.concatenate([jnp.broadcast_to(f_dist[:, 0:1], (DIFF_HEADS, PAGE)), f_dist[:, :dseq]], axis=1)
    bias_new = _t5_sample_rows(fz, PAGE, dseq)
    bias_cache = _t5_sample_rows(f_dist, past, dseq)
    npg = next(n for n in (16, 8, 4, 2, 1) if page_table.shape[1] % n == 0)
    oa_s, ob_s = _attn_sample(q_s, dq_s, kcat_s, dk_s, dv_s, bias_new, bias_cache, lam_vecs,
                              g_subln[l].reshape(1, -1), jnp.transpose(cache_mla[l], (0, 2, 1)),
                              cache_k[l], cache_v[l], page_table, npg)
    x1_s, h2_s, gates_s = _mix_out(xs, oa_s, ob_s, ga_s, gb_s, mod_s, 1, out_wts, tms)
    mod_s2 = mod_s.reshape(1, ts, 6 * D_MODEL) if ts <= 1024 else mod_s
    tmm_s = ts if ts <= 1024 else tms
    y_s = _moe(h2_s, gates_s, x1_s, mod_s2, 1, moe_wts, tmm_s)

    return (y_p.reshape(batch, seq, D_MODEL), y_s.reshape(nb, dseq, D_MODEL),
            lat_p.reshape(1, batch, seq, MLA_ROW), dk_p.reshape(1, batch, seq, DIFF_HEADS, 2 * DIFF_DH),
            dv_p.reshape(1, batch, seq, DIFF_HEADS, DIFF_DV),
            lat_s.reshape(1, nb, dseq, MLA_ROW), dk_s.reshape(1, nb, dseq, DIFF_HEADS, 2 * DIFF_DH),
            dv_s.reshape(1, nb, dseq, DIFF_HEADS, DIFF_DV))
```

```python
import functools
import math

import jax
import jax.numpy as jnp
from jax import lax
from jax.experimental import pallas as pl
from jax.experimental.pallas import tpu as pltpu

F32 = jnp.float32
BF16 = jnp.bfloat16

D_MODEL = 1024
PAGE = 128
MLA_HEADS = 8
MLA_NOPE = 64
MLA_ROPE = 32
Q_LORA = 256
KV_LORA = 128
MLA_ROW = KV_LORA + MLA_ROPE
MLA_SCALE = (MLA_NOPE + MLA_ROPE) ** -0.5
ROPE_THETA = 10000.0
DIFF_HEADS = 4
DIFF_DH = 64
DIFF_DV = 128
DIFF_SCALE = DIFF_DH ** -0.5
N_BUCKETS = 32
MAX_DISTANCE = 128
N_EXPERTS = 64
N_GROUPS = 8
TOPK_GROUPS = 4
TOP_K = 8
D_EXPERT = 256
D_SHARED = 256
ROUTE_SCALE = 2.5
EPS = 1e-6
NEG = -1e30
LAM_INIT = 0.8 - 0.6 * math.exp(-0.3 * 0)
LOG2E = math.log2(math.e)

QW = 256
W_IN_COLS = 4096
BQ = 256
MOE_EG = 4
VMEM_LIMIT = 56 * 1024 * 1024


def _cp(*sem):
    return pltpu.CompilerParams(dimension_semantics=sem, vmem_limit_bytes=VMEM_LIMIT)


def _dot(a, b):
    return jnp.dot(a, b, preferred_element_type=F32)


def _dot_nt(a, b):
    return lax.dot_general(a, b, (((1,), (1,)), ((), ())), preferred_element_type=F32)


def _rms(x, g):
    return x * lax.rsqrt(jnp.mean(x * x, axis=-1, keepdims=True) + EPS) * g


def _silu(x):
    return x * jax.nn.sigmoid(x)


def _lam(lq1_ref, lk1_ref, lq2_ref, lk2_ref):
    a = jnp.sum(lq1_ref[...] * lk1_ref[...], axis=-1, keepdims=True)
    b = jnp.sum(lq2_ref[...] * lk2_ref[...], axis=-1, keepdims=True)
    return jnp.exp(a) - jnp.exp(b) + LAM_INIT


def _bmm_kernel(a_ref, b_ref, o_ref):
    o_ref[...] = jnp.dot(a_ref[...], b_ref[...], preferred_element_type=F32, precision=lax.Precision.HIGHEST)


def _bmm(a, b):
    h, m, k = a.shape
    n = b.shape[2]
    return pl.pallas_call(
        _bmm_kernel,
        grid=(h,),
        in_specs=[pl.BlockSpec((None, m, k), lambda i: (i, 0, 0)), pl.BlockSpec((None, k, n), lambda i: (i, 0, 0))],
        out_specs=pl.BlockSpec((None, m, n), lambda i: (i, 0, 0)),
        out_shape=jax.ShapeDtypeStruct((h, m, n), F32),
        compiler_params=_cp("parallel"),
        name="fold_weights",
    )(a, b)


def _ada_kernel(c_ref, w_ref, b_ref, o_ref):
    c = c_ref[...]
    o_ref[...] = _dot(_silu(c).astype(BF16), w_ref[...].astype(BF16)) + b_ref[...]


def _ada(c, w, b):
    n = c.shape[0]
    return pl.pallas_call(
        _ada_kernel,
        grid=(6,),
        in_specs=[pl.BlockSpec((n, D_MODEL), lambda j: (0, 0)),
                  pl.BlockSpec((D_MODEL, D_MODEL), lambda j: (0, j)),
                  pl.BlockSpec((1, D_MODEL), lambda j: (0, j))],
        out_specs=pl.BlockSpec((n, D_MODEL), lambda j: (0, j)),
        out_shape=jax.ShapeDtypeStruct((n, 6 * D_MODEL), F32),
        compiler_params=_cp("parallel"),
        name="adaln",
    )(c, w, b.reshape(1, -1))


def _mod_spec(r, tiles_per_group, k, grid_rank=1):
    if grid_rank == 1:
        return pl.BlockSpec((None, r, D_MODEL), lambda i: (i // tiles_per_group, 0, k))
    return pl.BlockSpec((None, r, D_MODEL), lambda i, e: (i // tiles_per_group, 0, k))


def _rope128(v, c, s1, s2):
    return v * c + pltpu.roll(v, 112, 1) * s1 + pltpu.roll(v, 16, 1) * s2


def _in_kernel(x_ref, sh1_ref, sc1_ref, gmix_ref, win_ref, gq_ref, wq_ref, gkv_ref, c_ref, s1_ref, s2_ref,
               q_ref, kcat_ref, ckvt_ref, latt_ref, dq_ref, dk_ref, dv_ref, dkb_ref, dvt_ref, ga_ref, gb_ref):
    tm = x_ref.shape[0]
    x = x_ref[...]
    h = _rms(x, gmix_ref[...]) * (1.0 + sc1_ref[...]) + sh1_ref[...]
    hb = h.astype(BF16)
    c, s1, s2 = c_ref[...], s1_ref[...], s2_ref[...]

    qn = _rms(_dot(hb, win_ref[:, 0:256]), gq_ref[...]).astype(BF16)
    for hd in range(MLA_HEADS):
        qh = _dot(qn, wq_ref[:, hd * QW:(hd + 1) * QW])
        q_ref[:, hd * QW:hd * QW + 128] = qh[:, 0:128].astype(BF16)
        q_ref[:, hd * QW + 128:(hd + 1) * QW] = _rope128(qh[:, 128:256], c, s1, s2).astype(BF16)

    ckv = _rms(_dot(hb, win_ref[:, 256:384]), gkv_ref[...])
    kr = _rope128(_dot(hb, win_ref[:, 384:512]), c, s1, s2)
    ckv_t = ckv.T
    latt_ref[0:KV_LORA, :] = ckv_t
    latt_ref[KV_LORA:MLA_ROW, :] = kr.T[0:MLA_ROPE, :]
    kcat_ref[:, 0:128] = ckv.astype(BF16)
    kcat_ref[:, 128:256] = kr.astype(BF16)
    for cc in range(tm // BQ):
        ckvt_ref[cc] = ckv_t[:, cc * BQ:(cc + 1) * BQ].astype(BF16)

    dq_ref[...] = (_dot(hb, win_ref[:, 512:1024]) * (DIFF_SCALE * LOG2E)).astype(BF16)
    dk = _dot(hb, win_ref[:, 1024:1536])
    dkb_ref[...] = dk.astype(BF16)
    dv = _dot(hb, win_ref[:, 1536:2048])
    for hd in range(DIFF_HEADS):
        rows = pl.ds(hd, tm, stride=DIFF_HEADS)
        dk_ref[rows, :] = dk[:, hd * 128:(hd + 1) * 128]
        dv_ref[rows, :] = dv[:, hd * 128:(hd + 1) * 128]
    for cc in range(tm // BQ):
        for hd in range(DIFF_HEADS):
            dvt_ref[cc, hd] = dv[cc * BQ:(cc + 1) * BQ, hd * 128:(hd + 1) * 128].T.astype(BF16)
    ga_ref[...] = jax.nn.sigmoid(_dot(hb, win_ref[:, 2048:3072])).astype(BF16)
    gb_ref[...] = jax.nn.sigmoid(_dot(hb, win_ref[:, 3072:4096])).astype(BF16)


def _mixer_in(x, mod3, tiles_per_group, tabs, tab_blocks, wts, tm):
    t = x.shape[0]
    r = mod3.shape[1]
    nt = t // tm
    c_t, s1_t, s2_t = tabs
    g_mix, w_in_p, g_q, w_q, g_kv = wts
    const = lambda i: (0, 0)
    tab_spec = pl.BlockSpec((tm, 128), lambda i: (i % tab_blocks, 0))
    row = lambda w: pl.BlockSpec((tm, w), lambda i: (i, 0))
    nck = tm // BQ
    out_shape = (
        jax.ShapeDtypeStruct((t, MLA_HEADS * QW), BF16),
        jax.ShapeDtypeStruct((t, QW), BF16),
        jax.ShapeDtypeStruct((t // BQ, 128, BQ), BF16),
        jax.ShapeDtypeStruct((t // (tab_blocks * tm), MLA_ROW, tab_blocks * tm), F32),
        jax.ShapeDtypeStruct((t, 512), BF16),
        jax.ShapeDtypeStruct((t * DIFF_HEADS, 128), F32),
        jax.ShapeDtypeStruct((t * DIFF_HEADS, 128), F32),
        jax.ShapeDtypeStruct((t, 512), BF16),
        jax.ShapeDtypeStruct((t // BQ, DIFF_HEADS, 128, BQ), BF16),
        jax.ShapeDtypeStruct((t, D_MODEL), BF16),
        jax.ShapeDtypeStruct((t, D_MODEL), BF16),
    )
    out_specs = (
        row(MLA_HEADS * QW), row(QW),
        pl.BlockSpec((nck, 128, BQ), lambda i: (i, 0, 0)),
        pl.BlockSpec((None, MLA_ROW, tm), lambda i: (i // tab_blocks, 0, i % tab_blocks)),
        row(512),
        pl.BlockSpec((tm * DIFF_HEADS, 128), lambda i: (i, 0)),
        pl.BlockSpec((tm * DIFF_HEADS, 128), lambda i: (i, 0)),
        row(512),
        pl.BlockSpec((nck, DIFF_HEADS, 128, BQ), lambda i: (i, 0, 0, 0)),
        row(D_MODEL), row(D_MODEL),
    )
    return pl.pallas_call(
        _in_kernel,
        grid=(nt,),
        in_specs=[row(D_MODEL), _mod_spec(r, tiles_per_group, 0), _mod_spec(r, tiles_per_group, 1),
                  pl.BlockSpec((1, D_MODEL), const),
                  pl.BlockSpec((D_MODEL, W_IN_COLS), const),
                  pl.BlockSpec((1, Q_LORA), const),
                  pl.BlockSpec((Q_LORA, MLA_HEADS * QW), const),
                  pl.BlockSpec((1, KV_LORA), const),
                  tab_spec, tab_spec, tab_spec],
        out_specs=out_specs,
        out_shape=out_shape,
        compiler_params=_cp("parallel"),
        name="mixer_in",
    )(x, mod3, mod3, g_mix, w_in_p, g_q, w_q, g_kv, c_t, s1_t, s2_t)


def _online_t(s, m_ref, l_ref):
    m_prev = m_ref[...]
    m_new = jnp.maximum(m_prev, jnp.max(s, axis=0, keepdims=True))
    alpha = jnp.exp2(m_prev - m_new)
    p = jnp.exp2(s - m_new[0:1, :])
    l_ref[...] = alpha * l_ref[...] + jnp.sum(p, axis=0, keepdims=True)
    m_ref[...] = m_new
    return p.astype(BF16), alpha[0:1, :]


def _attn_prompt_kernel(q_ref, kc_ref, ckvt_ref, dq_ref, dk_ref, dvt_ref, bias_ref,
                        lq1_ref, lk1_ref, lq2_ref, lk2_ref, gsub_ref,
                        oa_ref, ob_ref, m_a, l_a, acc_a, m_d, l_d, acc_d, qst, qbd):
    i = pl.program_id(1)
    wd = 2 * BQ
    m_a[...] = jnp.full(m_a.shape, NEG, F32)
    l_a[...] = jnp.zeros(l_a.shape, F32)
    acc_a[...] = jnp.zeros(acc_a.shape, F32)
    m_d[...] = jnp.full(m_d.shape, NEG, F32)
    l_d[...] = jnp.zeros(l_d.shape, F32)
    acc_d[...] = jnp.zeros(acc_d.shape, F32)
    for hd in range(MLA_HEADS):
        qst[hd * BQ:(hd + 1) * BQ, :] = q_ref[:, hd * QW:(hd + 1) * QW]
    lane = lax.broadcasted_iota(jnp.int32, (BQ, 128), 1)
    for hd in range(DIFF_HEADS):
        dqh = dq_ref[:, hd * 128:(hd + 1) * 128].astype(F32)
        qbd[hd, 0:BQ, :] = jnp.where(lane < DIFF_DH, dqh, 0.0).astype(BF16)
        qbd[hd, BQ:wd, :] = jnp.where(lane >= DIFF_DH, dqh, 0.0).astype(BF16)

    def step(j, masked):
        k0 = pl.multiple_of(j * BQ, BQ)
        if masked:
            krow = lax.broadcasted_iota(jnp.int32, (BQ, BQ), 0)
            qcol = lax.broadcasted_iota(jnp.int32, (BQ, BQ), 1)
            keep = krow <= qcol
        s = _dot_nt(kc_ref[pl.ds(k0, BQ), :], qst[...])
        if masked:
            s = jnp.where(jnp.concatenate([keep] * MLA_HEADS, axis=1), s, NEG)
        p, alpha = _online_t(s, m_a, l_a)
        acc_a[...] = alpha * acc_a[...] + _dot(ckvt_ref[j], p)

        parts, biases = [], []
        for hd in range(DIFF_HEADS):
            parts.append(_dot_nt(dk_ref[pl.ds(k0, BQ), hd * 128:(hd + 1) * 128], qbd[hd]))
            b = bias_ref[hd, i - j]
            biases += [b, b]
        s = jnp.concatenate(parts, axis=1) + jnp.concatenate(biases, axis=1)
        if masked:
            s = jnp.where(jnp.concatenate([keep] * (2 * DIFF_HEADS), axis=1), s, NEG)
        p, alpha = _online_t(s, m_d, l_d)
        for hd in range(DIFF_HEADS):
            cs = slice(hd * wd, (hd + 1) * wd)
            acc_d[:, cs] = alpha[:, cs] * acc_d[:, cs] + _dot(dvt_ref[j, hd], p[:, cs])

    def body(j, carry):
        step(j, False)
        return carry

    lax.fori_loop(0, i, body, 0)
    step(i, True)

    ot = acc_a[...] / l_a[0:1, :]
    for hd in range(MLA_HEADS):
        oa_ref[:, hd * 128:(hd + 1) * 128] = ot[:, hd * BQ:(hd + 1) * BQ].T.astype(BF16)
    lam = _lam(lq1_ref, lk1_ref, lq2_ref, lk2_ref)
    od = acc_d[...] / l_d[0:1, :]
    for hd in range(DIFF_HEADS):
        ot = od[:, hd * wd:hd * wd + BQ] - lam * od[:, hd * wd + BQ:(hd + 1) * wd]
        ms = jnp.mean(ot * ot, axis=0, keepdims=True)
        ot = ot * lax.rsqrt(ms + EPS) * gsub_ref[...] * (1.0 - LAM_INIT)
        ob_ref[:, hd * 128:(hd + 1) * 128] = ot.T.astype(BF16)


def _attn_prompt(q, kcat, ckvt, dq, dkb, dvt, bias_t, lam_vecs, gsub_col, batch, seq):
    t = q.shape[0]
    nq = seq // BQ
    qrow = lambda w: pl.BlockSpec((BQ, w), lambda b, i: (b * nq + i, 0))
    vec = pl.BlockSpec((1, DIFF_DH), lambda b, i: (0, 0))
    return pl.pallas_call(
        _attn_prompt_kernel,
        grid=(batch, nq),
        in_specs=[qrow(MLA_HEADS * QW),
                  pl.BlockSpec((seq, QW), lambda b, i: (b, 0)),
                  pl.BlockSpec((nq, 128, BQ), lambda b, i: (b, 0, 0)),
                  qrow(512),
                  pl.BlockSpec((seq, 512), lambda b, i: (b, 0)),
                  pl.BlockSpec((nq, DIFF_HEADS, 128, BQ), lambda b, i: (b, 0, 0, 0)),
                  pl.BlockSpec((DIFF_HEADS, nq, BQ, BQ), lambda b, i: (0, 0, 0, 0)),
                  vec, vec, vec, vec,
                  pl.BlockSpec((DIFF_DV, 1), lambda b, i: (0, 0))],
        out_specs=(qrow(MLA_HEADS * 128), qrow(512)),
        out_shape=(jax.ShapeDtypeStruct((t, MLA_HEADS * 128), BF16), jax.ShapeDtypeStruct((t, 512), BF16)),
        scratch_shapes=[pltpu.VMEM((8, MLA_HEADS * BQ), F32), pltpu.VMEM((8, MLA_HEADS * BQ), F32),
                        pltpu.VMEM((128, MLA_HEADS * BQ), F32),
                        pltpu.VMEM((8, DIFF_HEADS * 2 * BQ), F32), pltpu.VMEM((8, DIFF_HEADS * 2 * BQ), F32),
                        pltpu.VMEM((128, DIFF_HEADS * 2 * BQ), F32),
                        pltpu.VMEM((MLA_HEADS * BQ, QW), BF16),
                        pltpu.VMEM((DIFF_HEADS, 2 * BQ, 128), BF16)],
        compiler_params=_cp("parallel", "arbitrary"),
        name="attn_prompt",
    )(q, kcat, ckvt, dq, dkb, dvt, bias_t, *lam_vecs, gsub_col)


def _attn_sample_kernel(npg, pt_ref, q_ref, dq_ref, kcn_ref, dkn_ref, dvn_ref, biasn_ref, bias_ref,
                        lq1_ref, lk1_ref, lq2_ref, lk2_ref, gsub_ref, *rest):
    lat_pages = rest[0:npg]
    k_pages = rest[npg:2 * npg]
    v_pages = rest[2 * npg:3 * npg]
    oa_ref, ob_ref = rest[3 * npg:3 * npg + 2]
    (lat0, k0, v0, pgl, pgk, pgv, qs, qbd, m_a, l_a, acc_a, m_d, l_d, acc_d) = rest[3 * npg + 2:]
    buf = (lat0, k0, v0)
    del pt_ref
    j = pl.program_id(1)
    nsteps = pl.num_programs(1)
    nrow = MLA_HEADS * 8

    def update(lat_t, kmat, vmat, bias, mask):
        s = _dot(qs[...].astype(BF16), lat_t)
        if mask is not None:
            s = jnp.where(mask, s, NEG)
        m_prev = m_a[...]
        m_new = jnp.maximum(m_prev, jnp.max(s, axis=1, keepdims=True))
        alpha = jnp.exp2(m_prev - m_new)
        p = jnp.exp2(s - m_new[:, 0:1])
        l_a[...] = alpha * l_a[...] + jnp.sum(p, axis=1, keepdims=True)
        m_a[...] = m_new
        acc_a[...] = alpha * acc_a[...] + _dot_nt(p.astype(BF16), lat_t[0:KV_LORA, :])
        s = _dot_nt(qbd[...].astype(BF16), kmat) + bias
        if mask is not None:
            s = jnp.where(mask, s, NEG)
        m_prev = m_d[...]
        m_new = jnp.maximum(m_prev, jnp.max(s, axis=1, keepdims=True))
        alpha = jnp.exp2(m_prev - m_new)
        p = jnp.exp2(s - m_new[:, 0:1])
        l_d[...] = alpha * l_d[...] + jnp.sum(p, axis=1, keepdims=True)
        m_d[...] = m_new
        acc_d[...] = alpha[:, 0:1] * acc_d[...] + _dot(p.astype(BF16), vmat)

    def cast_pages(buf):
        lat_b, k_b, v_b = buf
        for u in range(npg):
            lat_b[0:MLA_ROW, u * PAGE:(u + 1) * PAGE] = lat_pages[u][...].astype(BF16)
            for hd in range(DIFF_HEADS):
                rows = pl.ds(hd, PAGE, stride=DIFF_HEADS)
                k_b[u * PAGE:(u + 1) * PAGE, hd * 128:(hd + 1) * 128] = k_pages[u][rows, :].astype(BF16)
                v_b[u * PAGE:(u + 1) * PAGE, hd * 128:(hd + 1) * 128] = v_pages[u][rows, :].astype(BF16)

    @pl.when(j == 0)
    def _():
        m_a[...] = jnp.full(m_a.shape, NEG, F32)
        l_a[...] = jnp.zeros(l_a.shape, F32)
        acc_a[...] = jnp.zeros(acc_a.shape, F32)
        m_d[...] = jnp.full(m_d.shape, NEG, F32)
        l_d[...] = jnp.zeros(l_d.shape, F32)
        acc_d[...] = jnp.zeros(acc_d.shape, F32)
        lane = lax.broadcasted_iota(jnp.int32, (8, 512), 1)
        dq = dq_ref[...].astype(F32)
        for hd in range(MLA_HEADS):
            qs[hd * 8:(hd + 1) * 8, :] = q_ref[:, hd * QW:(hd + 1) * QW].astype(F32)
        for hd in range(DIFF_HEADS):
            for mp in range(2):
                lo = hd * 128 + mp * DIFF_DH
                r0 = (hd * 2 + mp) * 8
                qbd[r0:r0 + 8, :] = jnp.where((lane >= lo) & (lane < lo + DIFF_DH), dq, 0.0)
        pgl[...] = jnp.zeros(pgl.shape, F32)
        pgk[...] = jnp.zeros(pgk.shape, F32)
        pgv[...] = jnp.zeros(pgv.shape, F32)
        pgl[0:8, :] = kcn_ref[...].astype(F32)
        pgk[0:8, :] = dkn_ref[...]
        pgv[0:8, :] = dvn_ref[...]
        row = lax.broadcasted_iota(jnp.int32, (nrow, PAGE), 0)
        col = lax.broadcasted_iota(jnp.int32, (nrow, PAGE), 1)
        update(pgl[...].T.astype(BF16), pgk[...].astype(BF16), pgv[...].astype(BF16), biasn_ref[...],
               col <= (row % 8))
        lat0[MLA_ROW:QW, :] = jnp.zeros((QW - MLA_ROW, lat0.shape[1]), BF16)

    cast_pages(buf)
    update(lat0[...], k0[...], v0[...], bias_ref[...], None)

    @pl.when(j == nsteps - 1)
    def _():
        oa = acc_a[...] / l_a[...]
        for hd in range(MLA_HEADS):
            oa_ref[:, hd * 128:(hd + 1) * 128] = oa[hd * 8:(hd + 1) * 8, :].astype(BF16)
        lam = _lam(lq1_ref, lk1_ref, lq2_ref, lk2_ref)
        od = acc_d[...] / l_d[:, 0:1]
        for hd in range(DIFF_HEADS):
            o1 = od[hd * 16:hd * 16 + 8, hd * 128:(hd + 1) * 128]
            o2 = od[hd * 16 + 8:hd * 16 + 16, hd * 128:(hd + 1) * 128]
            o = _rms(o1 - lam * o2, gsub_ref[...]) * (1.0 - LAM_INIT)
            ob_ref[:, hd * 128:(hd + 1) * 128] = o.astype(BF16)


def _attn_sample(q, dq, kcat, dk, dv, bias_new, bias_cache, lam_vecs, gsub_row, cache_mla_t, cache_k, cache_v,
                 page_table, npg):
    nb = page_table.shape[0]
    nsteps = page_table.shape[1] // npg
    nk = npg * PAGE
    q3 = q.reshape(nb, 8, MLA_HEADS * QW)
    dq3 = dq.reshape(nb, 8, 512)
    kc3 = kcat.reshape(nb, 8, QW)
    dk3 = dk.reshape(nb, 8, 512)
    dv3 = dv.reshape(nb, 8, 512)
    prow = PAGE * DIFF_HEADS
    ck = cache_k.reshape(-1, 128)
    cv = cache_v.reshape(-1, 128)
    seqblk = lambda w: pl.BlockSpec((None, 8, w), lambda b, j, pt: (b, 0, 0))
    vec = pl.BlockSpec((1, DIFF_DH), lambda b, j, pt: (0, 0))

    def lat_spec(u):
        return pl.BlockSpec((None, MLA_ROW, PAGE), lambda b, j, pt: (pt[b, j * npg + u], 0, 0))

    def kv_spec(u):
        return pl.BlockSpec((prow, 128), lambda b, j, pt: (pt[b, j * npg + u], 0))

    in_specs = [seqblk(MLA_HEADS * QW), seqblk(512), seqblk(QW), seqblk(512), seqblk(512),
                pl.BlockSpec((64, PAGE), lambda b, j, pt: (0, 0)),
                pl.BlockSpec((64, nk), lambda b, j, pt: (0, j)),
                vec, vec, vec, vec,
                pl.BlockSpec((1, DIFF_DV), lambda b, j, pt: (0, 0))]
    in_specs += [lat_spec(u) for u in range(npg)]
    in_specs += [kv_spec(u) for u in range(npg)]
    in_specs += [kv_spec(u) for u in range(npg)]
    grid_spec = pltpu.PrefetchScalarGridSpec(
        num_scalar_prefetch=1,
        grid=(nb, nsteps),
        in_specs=in_specs,
        out_specs=(seqblk(MLA_HEADS * 128), seqblk(512)),
        scratch_shapes=[pltpu.VMEM((QW, nk), BF16), pltpu.VMEM((nk, 512), BF16), pltpu.VMEM((nk, 512), BF16),
                        pltpu.VMEM((PAGE, QW), F32), pltpu.VMEM((PAGE, 512), F32), pltpu.VMEM((PAGE, 512), F32),
                        pltpu.VMEM((64, QW), F32), pltpu.VMEM((64, 512), F32),
                        pltpu.VMEM((64, 128), F32), pltpu.VMEM((64, 128), F32), pltpu.VMEM((64, 128), F32),
                        pltpu.VMEM((64, 128), F32), pltpu.VMEM((64, 128), F32), pltpu.VMEM((64, 512), F32)],
    )
    oa, ob = pl.pallas_call(
        functools.partial(_attn_sample_kernel, npg),
        grid_spec=grid_spec,
        out_shape=(jax.ShapeDtypeStruct((nb, 8, MLA_HEADS * 128), BF16), jax.ShapeDtypeStruct((nb, 8, 512), BF16)),
        compiler_params=_cp("parallel", "arbitrary"),
        name="attn_sample",
    )(page_table, q3, dq3, kc3, dk3, dv3, bias_new, bias_cache, *lam_vecs, gsub_row,
      *([cache_mla_t] * npg), *([ck] * npg), *([cv] * npg))
    return oa.reshape(nb * 8, MLA_HEADS * 128), ob.reshape(nb * 8, 512)


def _first_max(v, idx, big):
    m = jnp.max(v, axis=0, keepdims=True)
    f = jnp.min(jnp.where(v == m, idx, big), axis=0, keepdims=True)
    return m, idx == f


def _mix_out_kernel(x_ref, oa_ref, ob_ref, ga_ref, gb_ref, gt1_ref, sh2_ref, sc2_ref, wova_ref, wb_ref, wo_ref, gffn_ref,
                    wrt_ref, rb_ref, x1_ref, h2_ref, gates_ref):
    tm = x_ref.shape[0]
    a = (ga_ref[...].astype(F32) * _dot(oa_ref[...], wova_ref[...])
         + gb_ref[...].astype(F32) * _dot(ob_ref[...], wb_ref[...]))
    x1 = x_ref[...] + gt1_ref[...] * _dot(a.astype(BF16), wo_ref[...])
    x1_ref[...] = x1
    h2 = _rms(x1, gffn_ref[...]) * (1.0 + sc2_ref[...]) + sh2_ref[...]
    h2_ref[...] = h2.astype(BF16)

    logits = lax.dot_general(wrt_ref[...], h2, (((1,), (1,)), ((), ())), preferred_element_type=F32,
                             precision=lax.Precision.HIGHEST)
    aff = jax.nn.sigmoid(logits)
    sel = aff + rb_ref[...]
    gsz = N_EXPERTS // N_GROUPS
    idx8 = lax.broadcasted_iota(jnp.int32, (gsz, tm), 0).astype(F32)
    scores = []
    for g in range(N_GROUPS):
        v = sel[g * gsz:(g + 1) * gsz, :]
        m1, hit = _first_max(v, idx8, float(gsz))
        m2 = jnp.max(jnp.where(hit, -jnp.inf, v), axis=0, keepdims=True)
        scores.append(m1 + m2)
    gs = jnp.concatenate(scores, axis=0)
    gidx = lax.broadcasted_iota(jnp.int32, (N_GROUPS, tm), 0).astype(F32)
    gsel = jnp.zeros((N_GROUPS, tm), F32)
    for _ in range(TOPK_GROUPS):
        _, hit = _first_max(gs, gidx, float(N_GROUPS))
        gsel = jnp.where(hit, 1.0, gsel)
        gs = jnp.where(hit, -jnp.inf, gs)
    cand = jnp.concatenate(
        [jnp.where(gsel[g:g + 1, :] > 0.0, sel[g * gsz:(g + 1) * gsz, :], -jnp.inf) for g in range(N_GROUPS)], axis=0)
    eidx = lax.broadcasted_iota(jnp.int32, (N_EXPERTS, tm), 0).astype(F32)
    w = jnp.zeros((N_EXPERTS, tm), F32)
    for _ in range(TOP_K):
        _, hit = _first_max(cand, eidx, float(N_EXPERTS))
        w = jnp.where(hit, aff, w)
        cand = jnp.where(hit, -jnp.inf, cand)
    gates_t = w / jnp.sum(w, axis=0, keepdims=True) * ROUTE_SCALE
    gates_ref[...] = jnp.concatenate([gates_t, jnp.zeros((128 - N_EXPERTS, tm), F32)], axis=0).T


def _mix_out(x, oa, ob, ga, gb, mod3, tiles_per_group, wts, tm):
    t = x.shape[0]
    r = mod3.shape[1]
    w_ova, w_b, w_o, g_ffn, w_rt, rb = wts
    const = lambda i: (0, 0)
    row = lambda w: pl.BlockSpec((tm, w), lambda i: (i, 0))
    return pl.pallas_call(
        _mix_out_kernel,
        grid=(t // tm,),
        in_specs=[row(D_MODEL), row(MLA_HEADS * 128), row(512), row(D_MODEL), row(D_MODEL),
                  _mod_spec(r, tiles_per_group, 2), _mod_spec(r, tiles_per_group, 3), _mod_spec(r, tiles_per_group, 4),
                  pl.BlockSpec((MLA_HEADS * 128, D_MODEL), const),
                  pl.BlockSpec((512, D_MODEL), const),
                  pl.BlockSpec((D_MODEL, D_MODEL), const),
                  pl.BlockSpec((1, D_MODEL), const),
                  pl.BlockSpec((N_EXPERTS, D_MODEL), const),
                  pl.BlockSpec((N_EXPERTS, 1), const)],
        out_specs=(row(D_MODEL), row(D_MODEL), row(128)),
        out_shape=(jax.ShapeDtypeStruct((t, D_MODEL), F32), jax.ShapeDtypeStruct((t, D_MODEL), BF16),
                   jax.ShapeDtypeStruct((t, 128), F32)),
        compiler_params=_cp("parallel"),
        name="mix_out",
    )(x, oa, ob, ga, gb, mod3, mod3, mod3, w_ova, w_b, w_o, g_ffn, w_rt, rb)


def _moe_kernel(h_ref, g_ref, wgu_ref, wd_ref, wsgu_ref, wsd_ref, x1_ref, gt2_ref, gfin_ref, o_ref, acc, hid_s):
    eg = pl.program_id(1)
    tm = h_ref.shape[0]
    h = h_ref[...]

    @pl.when(eg == 0)
    def _():
        sgu = _dot(h, wsgu_ref[...])
        hid = _silu(sgu[:, 0:D_SHARED]) * sgu[:, D_SHARED:2 * D_SHARED]
        acc[...] = _dot(hid.astype(BF16), wsd_ref[...])

    lane = lax.broadcasted_iota(jnp.int32, (tm, 128), 1)
    gates = g_ref[...]
    for k in range(MOE_EG):
        gu = _dot(h, wgu_ref[k])
        hid = _silu(gu[:, 0:D_EXPERT]) * gu[:, D_EXPERT:2 * D_EXPERT]
        g = jnp.sum(jnp.where(lane == eg * MOE_EG + k, gates, 0.0), axis=1, keepdims=True)
        hid_s[:, k * D_EXPERT:(k + 1) * D_EXPERT] = (hid * g).astype(BF16)
    acc[...] += _dot(hid_s[...], wd_ref[...].reshape(MOE_EG * D_EXPERT, D_MODEL))

    @pl.when(eg == pl.num_programs(1) - 1)
    def _():
        o_ref[...] = _rms(x1_ref[...] + gt2_ref[...] * acc[...], gfin_ref[...])


def _moe(h2, gates, x1, mod3, tiles_per_group, wts, tm):
    t = h2.shape[0]
    r = mod3.shape[1]
    w_gu, w_d, w_sgu, w_sd, g_fin = wts
    row = lambda w: pl.BlockSpec((tm, w), lambda i, e: (i, 0))
    const = lambda i, e: (0, 0)
    return pl.pallas_call(
        _moe_kernel,
        grid=(t // tm, N_EXPERTS // MOE_EG),
        in_specs=[row(D_MODEL), row(128),
                  pl.BlockSpec((MOE_EG, D_MODEL, 2 * D_EXPERT), lambda i, e: (e, 0, 0)),
                  pl.BlockSpec((MOE_EG, D_EXPERT, D_MODEL), lambda i, e: (e, 0, 0)),
                  pl.BlockSpec((D_MODEL, 2 * D_SHARED), const),
                  pl.BlockSpec((D_SHARED, D_MODEL), const),
                  row(D_MODEL), _mod_spec(r, tiles_per_group, 5, 2),
                  pl.BlockSpec((1, D_MODEL), const)],
        out_specs=row(D_MODEL),
        out_shape=jax.ShapeDtypeStruct((t, D_MODEL), F32),
        scratch_shapes=[pltpu.VMEM((tm, D_MODEL), F32), pltpu.VMEM((tm, MOE_EG * D_EXPERT), BF16)],
        compiler_params=_cp("parallel", "arbitrary"),
        name="moe",
    )(h2, gates, w_gu, w_d, w_sgu, w_sd, x1, mod3, g_fin)


def _t5_bucket(n):
    n = jnp.maximum(n, 0)
    max_exact = N_BUCKETS // 2
    nf = jnp.maximum(n, 1).astype(F32)
    large = max_exact + (jnp.log(nf / max_exact) / math.log(MAX_DISTANCE / max_exact)
                         * (N_BUCKETS - max_exact)).astype(jnp.int32)
    large = jnp.minimum(large, N_BUCKETS - 1)
    return jnp.where(n < max_exact, n, large)


def _t5_by_distance(rel_bias, nmax):
    hit = _t5_bucket(jnp.arange(nmax, dtype=jnp.int32))[:, None] == jnp.arange(N_BUCKETS, dtype=jnp.int32)[None, :]
    f = jnp.sum(jnp.where(hit[:, :, None], rel_bias[None].astype(F32), 0.0), axis=1)
    return f.T * LOG2E


def _t5_prompt_tiles(f, nq):
    h, c = f.shape[0], nq * BQ
    ln = c + BQ
    v = jnp.concatenate([f[:, :c], jnp.broadcast_to(f[:, 0:1], (h, BQ))], axis=1)
    m = jnp.tile(v, (1, BQ))[:, :BQ * (ln - 1)].reshape(h, BQ, ln - 1)[:, :, :c]
    return jnp.transpose(m.reshape(h, BQ, nq, BQ), (0, 2, 1, 3))


def _t5_sample_rows(f, past, dseq):
    rf = jnp.flip(f[:, :past + dseq], axis=1)
    per_t = jnp.stack([rf[:, dseq - 1 - t:dseq - 1 - t + past] for t in range(dseq)], axis=1)
    h = f.shape[0]
    return jnp.broadcast_to(per_t[:, None], (h, 2, dseq, past)).reshape(h * 2 * dseq, past)


def _rope_tables(pos):
    half = MLA_ROPE // 2
    inv = ROPE_THETA ** (-jnp.arange(half, dtype=F32) / half)
    ang = pos.astype(F32)[:, None] * inv[None, :]
    cos, sin = jnp.cos(ang), jnp.sin(ang)
    z = jnp.zeros((pos.shape[0], 128 - 2 * half), F32)
    zh = jnp.zeros_like(cos)
    return (jnp.concatenate([cos, cos, z], axis=1), jnp.concatenate([-sin, zh, z], axis=1),
            jnp.concatenate([zh, sin, z], axis=1))


def _tile(t, pref):
    while t % pref:
        pref //= 2
    return pref


def kernel(x_prompt, x_sample, c_prompt, c_sample, cache_mla, cache_k, cache_v, page_table, w_ada, b_ada, g_mix, w_in, g_q, w_uq, g_kv, w_uk, w_uv, lam_q1, lam_k1, lam_q2, lam_k2, g_subln, w_a, w_b, w_o, rel_bias, g_ffn, w_router, router_bias, w_exp_gu, w_exp_down, w_sh_gu, w_sh_down, g_final):
    batch, seq, _ = x_prompt.shape
    nb, dseq, _ = x_sample.shape
    past = page_table.shape[1] * PAGE
    l = 0
    tp, ts = batch * seq, nb * dseq

    w_in_l = w_in[l]
    w_in_p = jnp.concatenate([w_in_l[:, 0:416], jnp.zeros((D_MODEL, 96), F32), w_in_l[:, 416:]], axis=1).astype(BF16)
    uq = w_uq[l].reshape(Q_LORA, MLA_HEADS, MLA_NOPE + MLA_ROPE)
    w_ql = _bmm(jnp.transpose(uq[:, :, :MLA_NOPE], (1, 0, 2)), jnp.transpose(w_uk[l], (1, 2, 0)))
    w_qr = jnp.transpose(uq[:, :, MLA_NOPE:], (1, 0, 2))
    w_q = jnp.concatenate([w_ql, w_qr, jnp.zeros((MLA_HEADS, Q_LORA, QW - MLA_ROW), F32)], axis=2)
    w_q = w_q * (MLA_SCALE * LOG2E)
    w_q = jnp.transpose(w_q, (1, 0, 2)).reshape(Q_LORA, MLA_HEADS * QW).astype(BF16)
    w_ova = _bmm(jnp.transpose(w_uv[l], (1, 0, 2)), w_a[l].reshape(MLA_HEADS, 64, D_MODEL))
    w_ova = w_ova.reshape(MLA_HEADS * KV_LORA, D_MODEL).astype(BF16)
    in_wts = (g_mix[l].reshape(1, -1), w_in_p, g_q[l].reshape(1, -1), w_q, g_kv[l].reshape(1, -1))
    out_wts = (w_ova, w_b[l].astype(BF16), w_o[l].astype(BF16), g_ffn[l].reshape(1, -1),
               w_router[l].T, router_bias[l].reshape(-1, 1))
    moe_wts = (w_exp_gu[l].astype(BF16), w_exp_down[l].astype(BF16), w_sh_gu[l].astype(BF16),
               w_sh_down[l].astype(BF16), g_final.reshape(1, -1))
    lam_vecs = tuple(v[l].reshape(1, -1) for v in (lam_q1, lam_k1, lam_q2, lam_k2))

    mod = _ada(jnp.concatenate([c_prompt, c_sample], axis=0), w_ada[l], b_ada[l])
    mod_p = mod[:batch].reshape(batch, 1, 6 * D_MODEL)

    tm = _tile(seq, 512)
    xp = x_prompt.reshape(tp, D_MODEL)
    tabs_p = _rope_tables(jnp.arange(seq, dtype=jnp.int32))
    (q, kcat, ckvt, latt_p, dq, dk_p, dv_p, dkb, dvt, ga, gb) = _mixer_in(
        xp, mod_p, seq // tm, tabs_p, seq // tm, in_wts, tm)
    lat_p = jnp.transpose(latt_p, (0, 2, 1))
    nq = seq // BQ
    f_dist = _t5_by_distance(rel_bias, max(seq, past + dseq))
    bias_t = _t5_prompt_tiles(f_dist, nq)
    oa, ob = _attn_prompt(q, kcat, ckvt, dq, dkb, dvt, bias_t, lam_vecs, g_subln[l].reshape(-1, 1), batch, seq)
    x1, h2, gates = _mix_out(xp, oa, ob, ga, gb, mod_p, seq // tm, out_wts, tm)
    tmm = _tile(seq, 1024)
    y_p = _moe(h2, gates, x1, mod_p, seq // tmm, moe_wts, tmm)

    tms = _tile(ts, 256)
    xs = x_sample.reshape(ts, D_MODEL)
    mod_s = jnp.repeat(mod[batch:], dseq, axis=0).reshape(ts // tms, tms, 6 * D_MODEL)
    pos_s = past + (jnp.arange(ts, dtype=jnp.int32) % dseq)
    tabs_s = _rope_tables(pos_s)
    (q_s, kcat_s, _, latt_s, dq_s, dk_s, dv_s, _, _, ga_s, gb_s) = _mixer_in(
        xs, mod_s, 1, tabs_s, ts // tms, in_wts, tms)
    lat_s = jnp.transpose(latt_s[0], (1, 0))
    fz = jnp.concatenate([jnp.broadcast_to(f_dist[:, 0:1], (DIFF_HEADS, PAGE)), f_dist[:, :dseq]], axis=1)
    bias_new = _t5_sample_rows(fz, PAGE, dseq)
    bias_cache = _t5_sample_rows(f_dist, past, dseq)
    npg = next(n for n in (16, 8, 4, 2, 1) if page_table.shape[1] % n == 0)
    oa_s, ob_s = _attn_sample(q_s, dq_s, kcat_s, dk_s, dv_s, bias_new, bias_cache, lam_vecs,
                              g_subln[l].reshape(1, -1), jnp.transpose(cache_mla[l], (0, 2, 1)),
                              cache_k[l], cache_v[l], page_table, npg)
    x1_s, h2_s, gates_s = _mix_out(xs, oa_s, ob_s, ga_s, gb_s, mod_s, 1, out_wts, tms)
    tmm_s = _tile(ts, 512)
    y_s = _moe(h2_s, gates_s, x1_s, mod_s.reshape(ts // tmm_s, tmm_s, 6 * D_MODEL), 1, moe_wts, tmm_s)

    return (y_p.reshape(batch, seq, D_MODEL), y_s.reshape(nb, dseq, D_MODEL),
            lat_p.reshape(1, batch, seq, MLA_ROW), dk_p.reshape(1, batch, seq, DIFF_HEADS, 2 * DIFF_DH),
            dv_p.reshape(1, batch, seq, DIFF_HEADS, DIFF_DV),
            lat_s.reshape(1, nb, dseq, MLA_ROW), dk_s.reshape(1, nb, dseq, DIFF_HEADS, 2 * DIFF_DH),
            dv_s.reshape(1, nb, dseq, DIFF_HEADS, DIFF_DV))
```

```python
import functools
import math

import jax
import jax.numpy as jnp
from jax import lax
from jax.experimental import pallas as pl
from jax.experimental.pallas import tpu as pltpu

F32 = jnp.float32
BF16 = jnp.bfloat16

D_MODEL = 1024
PAGE = 128
MLA_HEADS = 8
MLA_NOPE = 64
MLA_ROPE = 32
Q_LORA = 256
KV_LORA = 128
MLA_ROW = KV_LORA + MLA_ROPE
MLA_SCALE = (MLA_NOPE + MLA_ROPE) ** -0.5
ROPE_THETA = 10000.0
DIFF_HEADS = 4
DIFF_DH = 64
DIFF_DV = 128
DIFF_SCALE = DIFF_DH ** -0.5
N_BUCKETS = 32
MAX_DISTANCE = 128
N_EXPERTS = 64
N_GROUPS = 8
TOPK_GROUPS = 4
TOP_K = 8
D_EXPERT = 256
D_SHARED = 256
ROUTE_SCALE = 2.5
EPS = 1e-6
NEG = -1e30
LAM_INIT = 0.8 - 0.6 * math.exp(-0.3 * 0)
LOG2E = math.log2(math.e)

QW = 256
W_IN_COLS = 4096
BQ = 256
MOE_EG = 4
VMEM_LIMIT = 56 * 1024 * 1024


def _cp(*sem):
    return pltpu.CompilerParams(dimension_semantics=sem, vmem_limit_bytes=VMEM_LIMIT)


def _dot(a, b):
    return jnp.dot(a, b, preferred_element_type=F32)


def _dot_nt(a, b):
    return lax.dot_general(a, b, (((1,), (1,)), ((), ())), preferred_element_type=F32)


def _rms(x, g):
    return x * lax.rsqrt(jnp.mean(x * x, axis=-1, keepdims=True) + EPS) * g


def _silu(x):
    return x * jax.nn.sigmoid(x)


def _lam(lq1_ref, lk1_ref, lq2_ref, lk2_ref):
    a = jnp.sum(lq1_ref[...] * lk1_ref[...], axis=-1, keepdims=True)
    b = jnp.sum(lq2_ref[...] * lk2_ref[...], axis=-1, keepdims=True)
    return jnp.exp(a) - jnp.exp(b) + LAM_INIT


def _bmm_kernel(a_ref, b_ref, o_ref):
    o_ref[...] = jnp.dot(a_ref[...], b_ref[...], preferred_element_type=F32, precision=lax.Precision.HIGHEST)


def _bmm(a, b):
    h, m, k = a.shape
    n = b.shape[2]
    return pl.pallas_call(
        _bmm_kernel,
        grid=(h,),
        in_specs=[pl.BlockSpec((None, m, k), lambda i: (i, 0, 0)), pl.BlockSpec((None, k, n), lambda i: (i, 0, 0))],
        out_specs=pl.BlockSpec((None, m, n), lambda i: (i, 0, 0)),
        out_shape=jax.ShapeDtypeStruct((h, m, n), F32),
        compiler_params=_cp("parallel"),
        name="fold_weights",
    )(a, b)


def _ada_kernel(c_ref, w_ref, b_ref, o_ref):
    c = c_ref[...]
    o_ref[...] = _dot(_silu(c).astype(BF16), w_ref[...].astype(BF16)) + b_ref[...]


def _ada(c, w, b):
    n = c.shape[0]
    return pl.pallas_call(
        _ada_kernel,
        grid=(6,),
        in_specs=[pl.BlockSpec((n, D_MODEL), lambda j: (0, 0)),
                  pl.BlockSpec((D_MODEL, D_MODEL), lambda j: (0, j)),
                  pl.BlockSpec((1, D_MODEL), lambda j: (0, j))],
        out_specs=pl.BlockSpec((n, D_MODEL), lambda j: (0, j)),
        out_shape=jax.ShapeDtypeStruct((n, 6 * D_MODEL), F32),
        compiler_params=_cp("parallel"),
        name="adaln",
    )(c, w, b.reshape(1, -1))


def _mod_spec(r, tiles_per_group, k, grid_rank=1):
    if grid_rank == 1:
        return pl.BlockSpec((None, r, D_MODEL), lambda i: (i // tiles_per_group, 0, k))
    return pl.BlockSpec((None, r, D_MODEL), lambda i, e: (i // tiles_per_group, 0, k))


def _rope128(v, c, s1, s2):
    return v * c + pltpu.roll(v, 112, 1) * s1 + pltpu.roll(v, 16, 1) * s2


def _in_kernel(x_ref, sh1_ref, sc1_ref, gmix_ref, win_ref, gq_ref, wq_ref, gkv_ref, c_ref, s1_ref, s2_ref,
               q_ref, kcat_ref, ckvt_ref, latt_ref, dq_ref, dk_ref, dv_ref, dkb_ref, dvt_ref, ga_ref, gb_ref):
    tm = x_ref.shape[0]
    x = x_ref[...]
    h = _rms(x, gmix_ref[...]) * (1.0 + sc1_ref[...]) + sh1_ref[...]
    hb = h.astype(BF16)
    c, s1, s2 = c_ref[...], s1_ref[...], s2_ref[...]

    qn = _rms(_dot(hb, win_ref[:, 0:256]), gq_ref[...]).astype(BF16)
    for hd in range(MLA_HEADS):
        qh = _dot(qn, wq_ref[:, hd * QW:(hd + 1) * QW])
        q_ref[:, hd * QW:hd * QW + 128] = qh[:, 0:128].astype(BF16)
        q_ref[:, hd * QW + 128:(hd + 1) * QW] = _rope128(qh[:, 128:256], c, s1, s2).astype(BF16)

    ckv = _rms(_dot(hb, win_ref[:, 256:384]), gkv_ref[...])
    kr = _rope128(_dot(hb, win_ref[:, 384:512]), c, s1, s2)
    ckv_t = ckv.T
    latt_ref[0:KV_LORA, :] = ckv_t
    latt_ref[KV_LORA:MLA_ROW, :] = kr.T[0:MLA_ROPE, :]
    kcat_ref[:, 0:128] = ckv.astype(BF16)
    kcat_ref[:, 128:256] = kr.astype(BF16)
    for cc in range(tm // BQ):
        ckvt_ref[cc] = ckv_t[:, cc * BQ:(cc + 1) * BQ].astype(BF16)

    dq_ref[...] = (_dot(hb, win_ref[:, 512:1024]) * (DIFF_SCALE * LOG2E)).astype(BF16)
    dk = _dot(hb, win_ref[:, 1024:1536])
    dkb_ref[...] = dk.astype(BF16)
    dv = _dot(hb, win_ref[:, 1536:2048])
    for hd in range(DIFF_HEADS):
        rows = pl.ds(hd, tm, stride=DIFF_HEADS)
        dk_ref[rows, :] = dk[:, hd * 128:(hd + 1) * 128]
        dv_ref[rows, :] = dv[:, hd * 128:(hd + 1) * 128]
    for cc in range(tm // BQ):
        for hd in range(DIFF_HEADS):
            dvt_ref[cc, hd] = dv[cc * BQ:(cc + 1) * BQ, hd * 128:(hd + 1) * 128].T.astype(BF16)
    ga_ref[...] = jax.nn.sigmoid(_dot(hb, win_ref[:, 2048:3072])).astype(BF16)
    gb_ref[...] = jax.nn.sigmoid(_dot(hb, win_ref[:, 3072:4096])).astype(BF16)


def _mixer_in(x, mod3, tiles_per_group, tabs, tab_blocks, wts, tm):
    t = x.shape[0]
    r = mod3.shape[1]
    nt = t // tm
    c_t, s1_t, s2_t = tabs
    g_mix, w_in_p, g_q, w_q, g_kv = wts
    const = lambda i: (0, 0)
    tab_spec = pl.BlockSpec((tm, 128), lambda i: (i % tab_blocks, 0))
    row = lambda w: pl.BlockSpec((tm, w), lambda i: (i, 0))
    nck = tm // BQ
    out_shape = (
        jax.ShapeDtypeStruct((t, MLA_HEADS * QW), BF16),
        jax.ShapeDtypeStruct((t, QW), BF16),
        jax.ShapeDtypeStruct((t // BQ, 128, BQ), BF16),
        jax.ShapeDtypeStruct((t // (tab_blocks * tm), MLA_ROW, tab_blocks * tm), F32),
        jax.ShapeDtypeStruct((t, 512), BF16),
        jax.ShapeDtypeStruct((t * DIFF_HEADS, 128), F32),
        jax.ShapeDtypeStruct((t * DIFF_HEADS, 128), F32),
        jax.ShapeDtypeStruct((t, 512), BF16),
        jax.ShapeDtypeStruct((t // BQ, DIFF_HEADS, 128, BQ), BF16),
        jax.ShapeDtypeStruct((t, D_MODEL), BF16),
        jax.ShapeDtypeStruct((t, D_MODEL), BF16),
    )
    out_specs = (
        row(MLA_HEADS * QW), row(QW),
        pl.BlockSpec((nck, 128, BQ), lambda i: (i, 0, 0)),
        pl.BlockSpec((None, MLA_ROW, tm), lambda i: (i // tab_blocks, 0, i % tab_blocks)),
        row(512),
        pl.BlockSpec((tm * DIFF_HEADS, 128), lambda i: (i, 0)),
        pl.BlockSpec((tm * DIFF_HEADS, 128), lambda i: (i, 0)),
        row(512),
        pl.BlockSpec((nck, DIFF_HEADS, 128, BQ), lambda i: (i, 0, 0, 0)),
        row(D_MODEL), row(D_MODEL),
    )
    return pl.pallas_call(
        _in_kernel,
        grid=(nt,),
        in_specs=[row(D_MODEL), _mod_spec(r, tiles_per_group, 0), _mod_spec(r, tiles_per_group, 1),
                  pl.BlockSpec((1, D_MODEL), const),
                  pl.BlockSpec((D_MODEL, W_IN_COLS), const),
                  pl.BlockSpec((1, Q_LORA), const),
                  pl.BlockSpec((Q_LORA, MLA_HEADS * QW), const),
                  pl.BlockSpec((1, KV_LORA), const),
                  tab_spec, tab_spec, tab_spec],
        out_specs=out_specs,
        out_shape=out_shape,
        compiler_params=_cp("parallel"),
        name="mixer_in",
    )(x, mod3, mod3, g_mix, w_in_p, g_q, w_q, g_kv, c_t, s1_t, s2_t)


def _online_t(s, m_ref, l_ref):
    m_prev = m_ref[...]
    m_new = jnp.maximum(m_prev, jnp.max(s, axis=0, keepdims=True))
    alpha = jnp.exp2(m_prev - m_new)
    p = jnp.exp2(s - m_new[0:1, :])
    l_ref[...] = alpha * l_ref[...] + jnp.sum(p, axis=0, keepdims=True)
    m_ref[...] = m_new
    return p.astype(BF16), alpha[0:1, :]


def _attn_prompt_kernel(q_ref, kc_ref, ckvt_ref, dq_ref, dk_ref, dvt_ref, bias_ref,
                        lq1_ref, lk1_ref, lq2_ref, lk2_ref, gsub_ref,
                        oa_ref, ob_ref, m_a, l_a, acc_a, m_d, l_d, acc_d, qst, qbd):
    i = pl.program_id(1)
    wd = 2 * BQ
    m_a[...] = jnp.full(m_a.shape, NEG, F32)
    l_a[...] = jnp.zeros(l_a.shape, F32)
    acc_a[...] = jnp.zeros(acc_a.shape, F32)
    m_d[...] = jnp.full(m_d.shape, NEG, F32)
    l_d[...] = jnp.zeros(l_d.shape, F32)
    acc_d[...] = jnp.zeros(acc_d.shape, F32)
    for hd in range(MLA_HEADS):
        qst[hd * BQ:(hd + 1) * BQ, :] = q_ref[:, hd * QW:(hd + 1) * QW]
    lane = lax.broadcasted_iota(jnp.int32, (BQ, 128), 1)
    for hd in range(DIFF_HEADS):
        dqh = dq_ref[:, hd * 128:(hd + 1) * 128].astype(F32)
        qbd[hd, 0:BQ, :] = jnp.where(lane < DIFF_DH, dqh, 0.0).astype(BF16)
        qbd[hd, BQ:wd, :] = jnp.where(lane >= DIFF_DH, dqh, 0.0).astype(BF16)

    def step(j, masked):
        k0 = pl.multiple_of(j * BQ, BQ)
        if masked:
            krow = lax.broadcasted_iota(jnp.int32, (BQ, BQ), 0)
            qcol = lax.broadcasted_iota(jnp.int32, (BQ, BQ), 1)
            keep = krow <= qcol
        s = _dot_nt(kc_ref[pl.ds(k0, BQ), :], qst[...])
        if masked:
            s = jnp.where(jnp.concatenate([keep] * MLA_HEADS, axis=1), s, NEG)
        p, alpha = _online_t(s, m_a, l_a)
        acc_a[...] = alpha * acc_a[...] + _dot(ckvt_ref[j], p)

        parts, biases = [], []
        for hd in range(DIFF_HEADS):
            parts.append(_dot_nt(dk_ref[pl.ds(k0, BQ), hd * 128:(hd + 1) * 128], qbd[hd]))
            b = bias_ref[hd, i - j]
            biases += [b, b]
        s = jnp.concatenate(parts, axis=1) + jnp.concatenate(biases, axis=1)
        if masked:
            s = jnp.where(jnp.concatenate([keep] * (2 * DIFF_HEADS), axis=1), s, NEG)
        p, alpha = _online_t(s, m_d, l_d)
        for hd in range(DIFF_HEADS):
            cs = slice(hd * wd, (hd + 1) * wd)
            acc_d[:, cs] = alpha[:, cs] * acc_d[:, cs] + _dot(dvt_ref[j, hd], p[:, cs])

    def body(j, carry):
        step(j, False)
        return carry

    lax.fori_loop(0, i, body, 0)
    step(i, True)

    ot = acc_a[...] / l_a[0:1, :]
    for hd in range(MLA_HEADS):
        oa_ref[:, hd * 128:(hd + 1) * 128] = ot[:, hd * BQ:(hd + 1) * BQ].T.astype(BF16)
    lam = _lam(lq1_ref, lk1_ref, lq2_ref, lk2_ref)
    od = acc_d[...] / l_d[0:1, :]
    for hd in range(DIFF_HEADS):
        ot = od[:, hd * wd:hd * wd + BQ] - lam * od[:, hd * wd + BQ:(hd + 1) * wd]
        ms = jnp.mean(ot * ot, axis=0, keepdims=True)
        ot = ot * lax.rsqrt(ms + EPS) * gsub_ref[...] * (1.0 - LAM_INIT)
        ob_ref[:, hd * 128:(hd + 1) * 128] = ot.T.astype(BF16)


def _attn_prompt(q, kcat, ckvt, dq, dkb, dvt, bias_t, lam_vecs, gsub_col, batch, seq):
    t = q.shape[0]
    nq = seq // BQ
    qrow = lambda w: pl.BlockSpec((BQ, w), lambda b, i: (b * nq + i, 0))
    vec = pl.BlockSpec((1, DIFF_DH), lambda b, i: (0, 0))
    return pl.pallas_call(
        _attn_prompt_kernel,
        grid=(batch, nq),
        in_specs=[qrow(MLA_HEADS * QW),
                  pl.BlockSpec((seq, QW), lambda b, i: (b, 0)),
                  pl.BlockSpec((nq, 128, BQ), lambda b, i: (b, 0, 0)),
                  qrow(512),
                  pl.BlockSpec((seq, 512), lambda b, i: (b, 0)),
                  pl.BlockSpec((nq, DIFF_HEADS, 128, BQ), lambda b, i: (b, 0, 0, 0)),
                  pl.BlockSpec((DIFF_HEADS, nq, BQ, BQ), lambda b, i: (0, 0, 0, 0)),
                  vec, vec, vec, vec,
                  pl.BlockSpec((DIFF_DV, 1), lambda b, i: (0, 0))],
        out_specs=(qrow(MLA_HEADS * 128), qrow(512)),
        out_shape=(jax.ShapeDtypeStruct((t, MLA_HEADS * 128), BF16), jax.ShapeDtypeStruct((t, 512), BF16)),
        scratch_shapes=[pltpu.VMEM((8, MLA_HEADS * BQ), F32), pltpu.VMEM((8, MLA_HEADS * BQ), F32),
                        pltpu.VMEM((128, MLA_HEADS * BQ), F32),
                        pltpu.VMEM((8, DIFF_HEADS * 2 * BQ), F32), pltpu.VMEM((8, DIFF_HEADS * 2 * BQ), F32),
                        pltpu.VMEM((128, DIFF_HEADS * 2 * BQ), F32),
                        pltpu.VMEM((MLA_HEADS * BQ, QW), BF16),
                        pltpu.VMEM((DIFF_HEADS, 2 * BQ, 128), BF16)],
        compiler_params=_cp("parallel", "arbitrary"),
        name="attn_prompt",
    )(q, kcat, ckvt, dq, dkb, dvt, bias_t, *lam_vecs, gsub_col)


def _attn_sample_kernel(npg, pt_ref, q_ref, dq_ref, kcn_ref, dkn_ref, dvn_ref, biasn_ref, bias_ref,
                        lq1_ref, lk1_ref, lq2_ref, lk2_ref, gsub_ref, lat_hbm, k_hbm, v_hbm, oa_ref, ob_ref,
                        latp, kp, vp, sem, lat0, k0, v0, pgl, pgk, pgv, qs, qbd, m_a, l_a, acc_a, m_d, l_d, acc_d):
    b = pl.program_id(0)
    j = pl.program_id(1)
    nsteps = pl.num_programs(1)
    step = b * nsteps + j
    slot = lax.rem(step, 2)
    nrow = MLA_HEADS * 8

    def page_copies(bb, jj, sl):
        cps = []
        for u in range(npg):
            page = pt_ref[bb, jj * npg + u]
            cps.append(pltpu.make_async_copy(lat_hbm.at[page], latp.at[sl, u], sem.at[sl]))
            cps.append(pltpu.make_async_copy(k_hbm.at[page], kp.at[sl, u], sem.at[sl]))
            cps.append(pltpu.make_async_copy(v_hbm.at[page], vp.at[sl, u], sem.at[sl]))
        return cps

    def start_all(cps):
        for n, cp in enumerate(cps):
            cp.start(priority=n % 2)

    @pl.when(step == 0)
    def _():
        start_all(page_copies(0, 0, 0))

    @pl.when(step + 1 < pl.num_programs(0) * nsteps)
    def _():
        last_j = j == nsteps - 1
        start_all(page_copies(jnp.where(last_j, b + 1, b), jnp.where(last_j, 0, j + 1), 1 - slot))

    for cp in page_copies(b, j, slot):
        cp.wait()

    def update(lat_t, kmat, vmat, bias, mask):
        s = _dot(qs[...].astype(BF16), lat_t)
        if mask is not None:
            s = jnp.where(mask, s, NEG)
        m_prev = m_a[...]
        m_new = jnp.maximum(m_prev, jnp.max(s, axis=1, keepdims=True))
        alpha = jnp.exp2(m_prev - m_new)
        p = jnp.exp2(s - m_new[:, 0:1])
        l_a[...] = alpha * l_a[...] + jnp.sum(p, axis=1, keepdims=True)
        m_a[...] = m_new
        acc_a[...] = alpha * acc_a[...] + _dot_nt(p.astype(BF16), lat_t[0:KV_LORA, :])
        s = _dot_nt(qbd[...].astype(BF16), kmat) + bias
        if mask is not None:
            s = jnp.where(mask, s, NEG)
        m_prev = m_d[...]
        m_new = jnp.maximum(m_prev, jnp.max(s, axis=1, keepdims=True))
        alpha = jnp.exp2(m_prev - m_new)
        p = jnp.exp2(s - m_new[:, 0:1])
        l_d[...] = alpha * l_d[...] + jnp.sum(p, axis=1, keepdims=True)
        m_d[...] = m_new
        acc_d[...] = alpha[:, 0:1] * acc_d[...] + _dot(p.astype(BF16), vmat)

    def cast_pages():
        for u in range(npg):
            lat0[0:MLA_ROW, u * PAGE:(u + 1) * PAGE] = latp[slot, u].astype(BF16)
            for hd in range(DIFF_HEADS):
                rows = pl.ds(hd, PAGE, stride=DIFF_HEADS)
                k0[u * PAGE:(u + 1) * PAGE, hd * 128:(hd + 1) * 128] = kp[slot, u, rows, :].astype(BF16)
                v0[u * PAGE:(u + 1) * PAGE, hd * 128:(hd + 1) * 128] = vp[slot, u, rows, :].astype(BF16)

    @pl.when(j == 0)
    def _():
        m_a[...] = jnp.full(m_a.shape, NEG, F32)
        l_a[...] = jnp.zeros(l_a.shape, F32)
        acc_a[...] = jnp.zeros(acc_a.shape, F32)
        m_d[...] = jnp.full(m_d.shape, NEG, F32)
        l_d[...] = jnp.zeros(l_d.shape, F32)
        acc_d[...] = jnp.zeros(acc_d.shape, F32)
        lane = lax.broadcasted_iota(jnp.int32, (8, 512), 1)
        dq = dq_ref[...].astype(F32)
        for hd in range(MLA_HEADS):
            qs[hd * 8:(hd + 1) * 8, :] = q_ref[:, hd * QW:(hd + 1) * QW].astype(F32)
        for hd in range(DIFF_HEADS):
            for mp in range(2):
                lo = hd * 128 + mp * DIFF_DH
                r0 = (hd * 2 + mp) * 8
                qbd[r0:r0 + 8, :] = jnp.where((lane >= lo) & (lane < lo + DIFF_DH), dq, 0.0)
        pgl[...] = jnp.zeros(pgl.shape, F32)
        pgk[...] = jnp.zeros(pgk.shape, F32)
        pgv[...] = jnp.zeros(pgv.shape, F32)
        pgl[0:8, :] = kcn_ref[...].astype(F32)
        pgk[0:8, :] = dkn_ref[...]
        pgv[0:8, :] = dvn_ref[...]
        row = lax.broadcasted_iota(jnp.int32, (nrow, PAGE), 0)
        col = lax.broadcasted_iota(jnp.int32, (nrow, PAGE), 1)
        update(pgl[...].T.astype(BF16), pgk[...].astype(BF16), pgv[...].astype(BF16), biasn_ref[...],
               col <= (row % 8))
        lat0[MLA_ROW:QW, :] = jnp.zeros((QW - MLA_ROW, lat0.shape[1]), BF16)

    cast_pages()
    update(lat0[...], k0[...], v0[...], bias_ref[...], None)

    @pl.when(j == nsteps - 1)
    def _():
        oa = acc_a[...] / l_a[...]
        for hd in range(MLA_HEADS):
            oa_ref[:, hd * 128:(hd + 1) * 128] = oa[hd * 8:(hd + 1) * 8, :].astype(BF16)
        lam = _lam(lq1_ref, lk1_ref, lq2_ref, lk2_ref)
        od = acc_d[...] / l_d[:, 0:1]
        for hd in range(DIFF_HEADS):
            o1 = od[hd * 16:hd * 16 + 8, hd * 128:(hd + 1) * 128]
            o2 = od[hd * 16 + 8:hd * 16 + 16, hd * 128:(hd + 1) * 128]
            o = _rms(o1 - lam * o2, gsub_ref[...]) * (1.0 - LAM_INIT)
            ob_ref[:, hd * 128:(hd + 1) * 128] = o.astype(BF16)


def _attn_sample(q, dq, kcat, dk, dv, bias_new, bias_cache, lam_vecs, gsub_row, cache_mla_t, cache_k, cache_v,
                 page_table, npg):
    nb = page_table.shape[0]
    nsteps = page_table.shape[1] // npg
    nk = npg * PAGE
    q3 = q.reshape(nb, 8, MLA_HEADS * QW)
    dq3 = dq.reshape(nb, 8, 512)
    kc3 = kcat.reshape(nb, 8, QW)
    dk3 = dk.reshape(nb, 8, 512)
    dv3 = dv.reshape(nb, 8, 512)
    prow = PAGE * DIFF_HEADS
    ck = cache_k.reshape(-1, prow, 128)
    cv = cache_v.reshape(-1, prow, 128)
    seqblk = lambda w: pl.BlockSpec((None, 8, w), lambda b, j, pt: (b, 0, 0))
    vec = pl.BlockSpec((1, DIFF_DH), lambda b, j, pt: (0, 0))
    hbm = pl.BlockSpec(memory_space=pl.ANY)
    in_specs = [seqblk(MLA_HEADS * QW), seqblk(512), seqblk(QW), seqblk(512), seqblk(512),
                pl.BlockSpec((64, PAGE), lambda b, j, pt: (0, 0)),
                pl.BlockSpec((64, nk), lambda b, j, pt: (0, j)),
                vec, vec, vec, vec,
                pl.BlockSpec((1, DIFF_DV), lambda b, j, pt: (0, 0)),
                hbm, hbm, hbm]
    grid_spec = pltpu.PrefetchScalarGridSpec(
        num_scalar_prefetch=1,
        grid=(nb, nsteps),
        in_specs=in_specs,
        out_specs=(seqblk(MLA_HEADS * 128), seqblk(512)),
        scratch_shapes=[pltpu.VMEM((2, npg, MLA_ROW, PAGE), F32), pltpu.VMEM((2, npg, prow, 128), F32),
                        pltpu.VMEM((2, npg, prow, 128), F32), pltpu.SemaphoreType.DMA((2,)),
                        pltpu.VMEM((QW, nk), BF16), pltpu.VMEM((nk, 512), BF16), pltpu.VMEM((nk, 512), BF16),
                        pltpu.VMEM((PAGE, QW), F32), pltpu.VMEM((PAGE, 512), F32), pltpu.VMEM((PAGE, 512), F32),
                        pltpu.VMEM((64, QW), F32), pltpu.VMEM((64, 512), F32),
                        pltpu.VMEM((64, 128), F32), pltpu.VMEM((64, 128), F32), pltpu.VMEM((64, 128), F32),
                        pltpu.VMEM((64, 128), F32), pltpu.VMEM((64, 128), F32), pltpu.VMEM((64, 512), F32)],
    )
    oa, ob = pl.pallas_call(
        functools.partial(_attn_sample_kernel, npg),
        grid_spec=grid_spec,
        out_shape=(jax.ShapeDtypeStruct((nb, 8, MLA_HEADS * 128), BF16), jax.ShapeDtypeStruct((nb, 8, 512), BF16)),
        compiler_params=_cp("arbitrary", "arbitrary"),
        name="attn_sample",
    )(page_table, q3, dq3, kc3, dk3, dv3, bias_new, bias_cache, *lam_vecs, gsub_row, cache_mla_t, ck, cv)
    return oa.reshape(nb * 8, MLA_HEADS * 128), ob.reshape(nb * 8, 512)


def _first_max(v, idx, big):
    m = jnp.max(v, axis=0, keepdims=True)
    f = jnp.min(jnp.where(v == m, idx, big), axis=0, keepdims=True)
    return m, idx == f


def _mix_out_kernel(x_ref, oa_ref, ob_ref, ga_ref, gb_ref, gt1_ref, sh2_ref, sc2_ref, wova_ref, wb_ref, wo_ref, gffn_ref,
                    wrt_ref, rb_ref, x1_ref, h2_ref, gates_ref):
    tm = x_ref.shape[0]
    a = (ga_ref[...].astype(F32) * _dot(oa_ref[...], wova_ref[...])
         + gb_ref[...].astype(F32) * _dot(ob_ref[...], wb_ref[...]))
    x1 = x_ref[...] + gt1_ref[...] * _dot(a.astype(BF16), wo_ref[...])
    x1_ref[...] = x1
    h2 = _rms(x1, gffn_ref[...]) * (1.0 + sc2_ref[...]) + sh2_ref[...]
    h2_ref[...] = h2.astype(BF16)

    logits = lax.dot_general(wrt_ref[...], h2, (((1,), (1,)), ((), ())), preferred_element_type=F32,
                             precision=lax.Precision.HIGHEST)
    aff = jax.nn.sigmoid(logits)
    sel = aff + rb_ref[...]
    gsz = N_EXPERTS // N_GROUPS
    idx8 = lax.broadcasted_iota(jnp.int32, (gsz, tm), 0).astype(F32)
    scores = []
    for g in range(N_GROUPS):
        v = sel[g * gsz:(g + 1) * gsz, :]
        m1, hit = _first_max(v, idx8, float(gsz))
        m2 = jnp.max(jnp.where(hit, -jnp.inf, v), axis=0, keepdims=True)
        scores.append(m1 + m2)
    gs = jnp.concatenate(scores, axis=0)
    gidx = lax.broadcasted_iota(jnp.int32, (N_GROUPS, tm), 0).astype(F32)
    gsel = jnp.zeros((N_GROUPS, tm), F32)
    for _ in range(TOPK_GROUPS):
        _, hit = _first_max(gs, gidx, float(N_GROUPS))
        gsel = jnp.where(hit, 1.0, gsel)
        gs = jnp.where(hit, -jnp.inf, gs)
    cand = jnp.concatenate(
        [jnp.where(gsel[g:g + 1, :] > 0.0, sel[g * gsz:(g + 1) * gsz, :], -jnp.inf) for g in range(N_GROUPS)], axis=0)
    eidx = lax.broadcasted_iota(jnp.int32, (N_EXPERTS, tm), 0).astype(F32)
    w = jnp.zeros((N_EXPERTS, tm), F32)
    for _ in range(TOP_K):
        _, hit = _first_max(cand, eidx, float(N_EXPERTS))
        w = jnp.where(hit, aff, w)
        cand = jnp.where(hit, -jnp.inf, cand)
    gates_t = w / jnp.sum(w, axis=0, keepdims=True) * ROUTE_SCALE
    gates_ref[...] = jnp.concatenate([gates_t, jnp.zeros((128 - N_EXPERTS, tm), F32)], axis=0).T


def _mix_out(x, oa, ob, ga, gb, mod3, tiles_per_group, wts, tm):
    t = x.shape[0]
    r = mod3.shape[1]
    w_ova, w_b, w_o, g_ffn, w_rt, rb = wts
    const = lambda i: (0, 0)
    row = lambda w: pl.BlockSpec((tm, w), lambda i: (i, 0))
    return pl.pallas_call(
        _mix_out_kernel,
        grid=(t // tm,),
        in_specs=[row(D_MODEL), row(MLA_HEADS * 128), row(512), row(D_MODEL), row(D_MODEL),
                  _mod_spec(r, tiles_per_group, 2), _mod_spec(r, tiles_per_group, 3), _mod_spec(r, tiles_per_group, 4),
                  pl.BlockSpec((MLA_HEADS * 128, D_MODEL), const),
                  pl.BlockSpec((512, D_MODEL), const),
                  pl.BlockSpec((D_MODEL, D_MODEL), const),
                  pl.BlockSpec((1, D_MODEL), const),
                  pl.BlockSpec((N_EXPERTS, D_MODEL), const),
                  pl.BlockSpec((N_EXPERTS, 1), const)],
        out_specs=(row(D_MODEL), row(D_MODEL), row(128)),
        out_shape=(jax.ShapeDtypeStruct((t, D_MODEL), F32), jax.ShapeDtypeStruct((t, D_MODEL), BF16),
                   jax.ShapeDtypeStruct((t, 128), F32)),
        compiler_params=_cp("parallel"),
        name="mix_out",
    )(x, oa, ob, ga, gb, mod3, mod3, mod3, w_ova, w_b, w_o, g_ffn, w_rt, rb)


def _moe_kernel(h_ref, g_ref, wgu_ref, wd_ref, wsgu_ref, wsd_ref, x1_ref, gt2_ref, gfin_ref, o_ref, acc, hid_s):
    eg = pl.program_id(1)
    tm = h_ref.shape[0]
    h = h_ref[...]

    @pl.when(eg == 0)
    def _():
        sgu = _dot(h, wsgu_ref[...])
        hid = _silu(sgu[:, 0:D_SHARED]) * sgu[:, D_SHARED:2 * D_SHARED]
        acc[...] = _dot(hid.astype(BF16), wsd_ref[...])

    lane = lax.broadcasted_iota(jnp.int32, (tm, 128), 1)
    gates = g_ref[...]
    for k in range(MOE_EG):
        gu = _dot(h, wgu_ref[k])
        hid = _silu(gu[:, 0:D_EXPERT]) * gu[:, D_EXPERT:2 * D_EXPERT]
        g = jnp.sum(jnp.where(lane == eg * MOE_EG + k, gates, 0.0), axis=1, keepdims=True)
        hid_s[:, k * D_EXPERT:(k + 1) * D_EXPERT] = (hid * g).astype(BF16)
    acc[...] += _dot(hid_s[...], wd_ref[...].reshape(MOE_EG * D_EXPERT, D_MODEL))

    @pl.when(eg == pl.num_programs(1) - 1)
    def _():
        o_ref[...] = _rms(x1_ref[...] + gt2_ref[...] * acc[...], gfin_ref[...])


def _moe(h2, gates, x1, mod3, tiles_per_group, wts, tm):
    t = h2.shape[0]
    r = mod3.shape[1]
    w_gu, w_d, w_sgu, w_sd, g_fin = wts
    row = lambda w: pl.BlockSpec((tm, w), lambda i, e: (i, 0))
    const = lambda i, e: (0, 0)
    return pl.pallas_call(
        _moe_kernel,
        grid=(t // tm, N_EXPERTS // MOE_EG),
        in_specs=[row(D_MODEL), row(128),
                  pl.BlockSpec((MOE_EG, D_MODEL, 2 * D_EXPERT), lambda i, e: (e, 0, 0)),
                  pl.BlockSpec((MOE_EG, D_EXPERT, D_MODEL), lambda i, e: (e, 0, 0)),
                  pl.BlockSpec((D_MODEL, 2 * D_SHARED), const),
                  pl.BlockSpec((D_SHARED, D_MODEL), const),
                  row(D_MODEL), _mod_spec(r, tiles_per_group, 5, 2),
                  pl.BlockSpec((1, D_MODEL), const)],
        out_specs=row(D_MODEL),
        out_shape=jax.ShapeDtypeStruct((t, D_MODEL), F32),
        scratch_shapes=[pltpu.VMEM((tm, D_MODEL), F32), pltpu.VMEM((tm, MOE_EG * D_EXPERT), BF16)],
        compiler_params=_cp("parallel", "arbitrary"),
        name="moe",
    )(h2, gates, w_gu, w_d, w_sgu, w_sd, x1, mod3, g_fin)


def _t5_bucket(n):
    n = jnp.maximum(n, 0)
    max_exact = N_BUCKETS // 2
    nf = jnp.maximum(n, 1).astype(F32)
    large = max_exact + (jnp.log(nf / max_exact) / math.log(MAX_DISTANCE / max_exact)
                         * (N_BUCKETS - max_exact)).astype(jnp.int32)
    large = jnp.minimum(large, N_BUCKETS - 1)
    return jnp.where(n < max_exact, n, large)


def _t5_by_distance(rel_bias, nmax):
    hit = _t5_bucket(jnp.arange(nmax, dtype=jnp.int32))[:, None] == jnp.arange(N_BUCKETS, dtype=jnp.int32)[None, :]
    f = jnp.sum(jnp.where(hit[:, :, None], rel_bias[None].astype(F32), 0.0), axis=1)
    return f.T * LOG2E


def _t5_prompt_tiles(f, nq):
    h, c = f.shape[0], nq * BQ
    ln = c + BQ
    v = jnp.concatenate([f[:, :c], jnp.broadcast_to(f[:, 0:1], (h, BQ))], axis=1)
    m = jnp.tile(v, (1, BQ))[:, :BQ * (ln - 1)].reshape(h, BQ, ln - 1)[:, :, :c]
    return jnp.transpose(m.reshape(h, BQ, nq, BQ), (0, 2, 1, 3))


def _t5_sample_rows(f, past, dseq):
    rf = jnp.flip(f[:, :past + dseq], axis=1)
    per_t = jnp.stack([rf[:, dseq - 1 - t:dseq - 1 - t + past] for t in range(dseq)], axis=1)
    h = f.shape[0]
    return jnp.broadcast_to(per_t[:, None], (h, 2, dseq, past)).reshape(h * 2 * dseq, past)


def _rope_tables(pos):
    half = MLA_ROPE // 2
    inv = ROPE_THETA ** (-jnp.arange(half, dtype=F32) / half)
    ang = pos.astype(F32)[:, None] * inv[None, :]
    cos, sin = jnp.cos(ang), jnp.sin(ang)
    z = jnp.zeros((pos.shape[0], 128 - 2 * half), F32)
    zh = jnp.zeros_like(cos)
    return (jnp.concatenate([cos, cos, z], axis=1), jnp.concatenate([-sin, zh, z], axis=1),
            jnp.concatenate([zh, sin, z], axis=1))


def _tile(t, pref):
    while t % pref:
        pref //= 2
    return pref


def kernel(x_prompt, x_sample, c_prompt, c_sample, cache_mla, cache_k, cache_v, page_table, w_ada, b_ada, g_mix, w_in, g_q, w_uq, g_kv, w_uk, w_uv, lam_q1, lam_k1, lam_q2, lam_k2, g_subln, w_a, w_b, w_o, rel_bias, g_ffn, w_router, router_bias, w_exp_gu, w_exp_down, w_sh_gu, w_sh_down, g_final):
    batch, seq, _ = x_prompt.shape
    nb, dseq, _ = x_sample.shape
    past = page_table.shape[1] * PAGE
    l = 0
    tp, ts = batch * seq, nb * dseq

    w_in_l = w_in[l]
    w_in_p = jnp.concatenate([w_in_l[:, 0:416], jnp.zeros((D_MODEL, 96), F32), w_in_l[:, 416:]], axis=1).astype(BF16)
    uq = w_uq[l].reshape(Q_LORA, MLA_HEADS, MLA_NOPE + MLA_ROPE)
    w_ql = _bmm(jnp.transpose(uq[:, :, :MLA_NOPE], (1, 0, 2)), jnp.transpose(w_uk[l], (1, 2, 0)))
    w_qr = jnp.transpose(uq[:, :, MLA_NOPE:], (1, 0, 2))
    w_q = jnp.concatenate([w_ql, w_qr, jnp.zeros((MLA_HEADS, Q_LORA, QW - MLA_ROW), F32)], axis=2)
    w_q = w_q * (MLA_SCALE * LOG2E)
    w_q = jnp.transpose(w_q, (1, 0, 2)).reshape(Q_LORA, MLA_HEADS * QW).astype(BF16)
    w_ova = _bmm(jnp.transpose(w_uv[l], (1, 0, 2)), w_a[l].reshape(MLA_HEADS, 64, D_MODEL))
    w_ova = w_ova.reshape(MLA_HEADS * KV_LORA, D_MODEL).astype(BF16)
    in_wts = (g_mix[l].reshape(1, -1), w_in_p, g_q[l].reshape(1, -1), w_q, g_kv[l].reshape(1, -1))
    out_wts = (w_ova, w_b[l].astype(BF16), w_o[l].astype(BF16), g_ffn[l].reshape(1, -1),
               w_router[l].T, router_bias[l].reshape(-1, 1))
    moe_wts = (w_exp_gu[l].astype(BF16), w_exp_down[l].astype(BF16), w_sh_gu[l].astype(BF16),
               w_sh_down[l].astype(BF16), g_final.reshape(1, -1))
    lam_vecs = tuple(v[l].reshape(1, -1) for v in (lam_q1, lam_k1, lam_q2, lam_k2))

    mod = _ada(jnp.concatenate([c_prompt, c_sample], axis=0), w_ada[l], b_ada[l])
    mod_p = mod[:batch].reshape(batch, 1, 6 * D_MODEL)

    tm = _tile(seq, 512)
    xp = x_prompt.reshape(tp, D_MODEL)
    tabs_p = _rope_tables(jnp.arange(seq, dtype=jnp.int32))
    (q, kcat, ckvt, latt_p, dq, dk_p, dv_p, dkb, dvt, ga, gb) = _mixer_in(
        xp, mod_p, seq // tm, tabs_p, seq // tm, in_wts, tm)
    lat_p = jnp.transpose(latt_p, (0, 2, 1))
    nq = seq // BQ
    f_dist = _t5_by_distance(rel_bias, max(seq, past + dseq))
    bias_t = _t5_prompt_tiles(f_dist, nq)
    oa, ob = _attn_prompt(q, kcat, ckvt, dq, dkb, dvt, bias_t, lam_vecs, g_subln[l].reshape(-1, 1), batch, seq)
    x1, h2, gates = _mix_out(xp, oa, ob, ga, gb, mod_p, seq // tm, out_wts, tm)
    tmm = _tile(seq, 1024)
    y_p = _moe(h2, gates, x1, mod_p, seq // tmm, moe_wts, tmm)

    tms = _tile(ts, 256)
    xs = x_sample.reshape(ts, D_MODEL)
    mod_s = jnp.repeat(mod[batch:], dseq, axis=0).reshape(ts // tms, tms, 6 * D_MODEL)
    pos_s = past + (jnp.arange(ts, dtype=jnp.int32) % dseq)
    tabs_s = _rope_tables(pos_s)
    (q_s, kcat_s, _, latt_s, dq_s, dk_s, dv_s, _, _, ga_s, gb_s) = _mixer_in(
        xs, mod_s, 1, tabs_s, ts // tms, in_wts, tms)
    lat_s = jnp.transpose(latt_s[0], (1, 0))
    fz = jnp.concatenate([jnp.broadcast_to(f_dist[:, 0:1], (DIFF_HEADS, PAGE)), f_dist[:, :dseq]], axis=1)
    bias_new = _t5_sample_rows(fz, PAGE, dseq)
    bias_cache = _t5_sample_rows(f_dist, past, dseq)
    npg = next(n for n in (16, 8, 4, 2, 1) if page_table.shape[1] % n == 0)
    oa_s, ob_s = _attn_sample(q_s, dq_s, kcat_s, dk_s, dv_s, bias_new, bias_cache, lam_vecs,
                              g_subln[l].reshape(1, -1), jnp.transpose(cache_mla[l], (0, 2, 1)),
                              cache_k[l], cache_v[l], page_table, npg)
    x1_s, h2_s, gates_s = _mix_out(xs, oa_s, ob_s, ga_s, gb_s, mod_s, 1, out_wts, tms)
    tmm_s = _tile(ts, 512)
    y_s = _moe(h2_s, gates_s, x1_s, mod_s.reshape(ts // tmm_s, tmm_s, 6 * D_MODEL), 1, moe_wts, tmm_s)

    return (y_p.reshape(batch, seq, D_MODEL), y_s.reshape(nb, dseq, D_MODEL),
            lat_p.reshape(1, batch, seq, MLA_ROW), dk_p.reshape(1, batch, seq, DIFF_HEADS, 2 * DIFF_DH),
            dv_p.reshape(1, batch, seq, DIFF_HEADS, DIFF_DV),
            lat_s.reshape(1, nb, dseq, MLA_ROW), dk_s.reshape(1, nb, dseq, DIFF_HEADS, 2 * DIFF_DH),
            dv_s.reshape(1, nb, dseq, DIFF_HEADS, DIFF_DV))
```

```python
import functools
import math

import jax
import jax.numpy as jnp
from jax import lax
from jax.experimental import pallas as pl
from jax.experimental.pallas import tpu as pltpu

F32 = jnp.float32
BF16 = jnp.bfloat16

D_MODEL = 1024
PAGE = 128
MLA_HEADS = 8
MLA_NOPE = 64
MLA_ROPE = 32
Q_LORA = 256
KV_LORA = 128
MLA_ROW = KV_LORA + MLA_ROPE
MLA_SCALE = (MLA_NOPE + MLA_ROPE) ** -0.5
ROPE_THETA = 10000.0
DIFF_HEADS = 4
DIFF_DH = 64
DIFF_DV = 128
DIFF_SCALE = DIFF_DH ** -0.5
N_BUCKETS = 32
MAX_DISTANCE = 128
N_EXPERTS = 64
N_GROUPS = 8
TOPK_GROUPS = 4
TOP_K = 8
D_EXPERT = 256
D_SHARED = 256
ROUTE_SCALE = 2.5
EPS = 1e-6
NEG = -1e30
LAM_INIT = 0.8 - 0.6 * math.exp(-0.3 * 0)
LOG2E = math.log2(math.e)

QW = 256
W_IN_COLS = 4096
BQ = 256
MOE_EG = 4
RING = 3
VMEM_LIMIT = 56 * 1024 * 1024


def _cp(*sem):
    return pltpu.CompilerParams(dimension_semantics=sem, vmem_limit_bytes=VMEM_LIMIT)


def _dot(a, b):
    return jnp.dot(a, b, preferred_element_type=F32)


def _dot_nt(a, b):
    return lax.dot_general(a, b, (((1,), (1,)), ((), ())), preferred_element_type=F32)


def _rms(x, g):
    return x * lax.rsqrt(jnp.mean(x * x, axis=-1, keepdims=True) + EPS) * g


def _silu(x):
    return x * jax.nn.sigmoid(x)


def _lam(lq1_ref, lk1_ref, lq2_ref, lk2_ref):
    a = jnp.sum(lq1_ref[...] * lk1_ref[...], axis=-1, keepdims=True)
    b = jnp.sum(lq2_ref[...] * lk2_ref[...], axis=-1, keepdims=True)
    return jnp.exp(a) - jnp.exp(b) + LAM_INIT


def _bmm_kernel(a_ref, b_ref, o_ref):
    o_ref[...] = jnp.dot(a_ref[...], b_ref[...], preferred_element_type=F32, precision=lax.Precision.HIGHEST)


def _bmm(a, b):
    h, m, k = a.shape
    n = b.shape[2]
    return pl.pallas_call(
        _bmm_kernel,
        grid=(h,),
        in_specs=[pl.BlockSpec((None, m, k), lambda i: (i, 0, 0)), pl.BlockSpec((None, k, n), lambda i: (i, 0, 0))],
        out_specs=pl.BlockSpec((None, m, n), lambda i: (i, 0, 0)),
        out_shape=jax.ShapeDtypeStruct((h, m, n), F32),
        compiler_params=_cp("parallel"),
        name="fold_weights",
    )(a, b)


def _ada_kernel(c_ref, w_ref, b_ref, o_ref):
    c = c_ref[...]
    o_ref[...] = _dot(_silu(c).astype(BF16), w_ref[...].astype(BF16)) + b_ref[...]


def _ada(c, w, b):
    n = c.shape[0]
    return pl.pallas_call(
        _ada_kernel,
        grid=(6,),
        in_specs=[pl.BlockSpec((n, D_MODEL), lambda j: (0, 0)),
                  pl.BlockSpec((D_MODEL, D_MODEL), lambda j: (0, j)),
                  pl.BlockSpec((1, D_MODEL), lambda j: (0, j))],
        out_specs=pl.BlockSpec((n, D_MODEL), lambda j: (0, j)),
        out_shape=jax.ShapeDtypeStruct((n, 6 * D_MODEL), F32),
        compiler_params=_cp("parallel"),
        name="adaln",
    )(c, w, b.reshape(1, -1))


def _mod_spec(r, tiles_per_group, k, grid_rank=1):
    if grid_rank == 1:
        return pl.BlockSpec((None, r, D_MODEL), lambda i: (i // tiles_per_group, 0, k))
    return pl.BlockSpec((None, r, D_MODEL), lambda i, e: (i // tiles_per_group, 0, k))


def _rope128(v, c, s1, s2):
    return v * c + pltpu.roll(v, 112, 1) * s1 + pltpu.roll(v, 16, 1) * s2


def _in_kernel(x_ref, sh1_ref, sc1_ref, gmix_ref, win_ref, gq_ref, wq_ref, gkv_ref, c_ref, s1_ref, s2_ref,
               q_ref, kcat_ref, ckvt_ref, latt_ref, dq_ref, dk_ref, dv_ref, dkb_ref, dvt_ref, ga_ref, gb_ref):
    tm = x_ref.shape[0]
    x = x_ref[...]
    h = _rms(x, gmix_ref[...]) * (1.0 + sc1_ref[...]) + sh1_ref[...]
    hb = h.astype(BF16)
    c, s1, s2 = c_ref[...], s1_ref[...], s2_ref[...]

    qn = _rms(_dot(hb, win_ref[:, 0:256]), gq_ref[...]).astype(BF16)
    for hd in range(MLA_HEADS):
        qh = _dot(qn, wq_ref[:, hd * QW:(hd + 1) * QW])
        q_ref[:, hd * QW:hd * QW + 128] = qh[:, 0:128].astype(BF16)
        q_ref[:, hd * QW + 128:(hd + 1) * QW] = _rope128(qh[:, 128:256], c, s1, s2).astype(BF16)

    ckv = _rms(_dot(hb, win_ref[:, 256:384]), gkv_ref[...])
    kr = _rope128(_dot(hb, win_ref[:, 384:512]), c, s1, s2)
    ckv_t = ckv.T
    latt_ref[0:KV_LORA, :] = ckv_t
    latt_ref[KV_LORA:MLA_ROW, :] = kr.T[0:MLA_ROPE, :]
    kcat_ref[:, 0:128] = ckv.astype(BF16)
    kcat_ref[:, 128:256] = kr.astype(BF16)
    for cc in range(tm // BQ):
        ckvt_ref[cc] = ckv_t[:, cc * BQ:(cc + 1) * BQ].astype(BF16)

    dq_ref[...] = (_dot(hb, win_ref[:, 512:1024]) * (DIFF_SCALE * LOG2E)).astype(BF16)
    dk = _dot(hb, win_ref[:, 1024:1536])
    dkb_ref[...] = dk.astype(BF16)
    dv = _dot(hb, win_ref[:, 1536:2048])
    for hd in range(DIFF_HEADS):
        rows = pl.ds(hd, tm, stride=DIFF_HEADS)
        dk_ref[rows, :] = dk[:, hd * 128:(hd + 1) * 128]
        dv_ref[rows, :] = dv[:, hd * 128:(hd + 1) * 128]
    for cc in range(tm // BQ):
        for hd in range(DIFF_HEADS):
            dvt_ref[cc, hd] = dv[cc * BQ:(cc + 1) * BQ, hd * 128:(hd + 1) * 128].T.astype(BF16)
    ga_ref[...] = jax.nn.sigmoid(_dot(hb, win_ref[:, 2048:3072])).astype(BF16)
    gb_ref[...] = jax.nn.sigmoid(_dot(hb, win_ref[:, 3072:4096])).astype(BF16)


def _mixer_in(x, mod3, tiles_per_group, tabs, tab_blocks, wts, tm):
    t = x.shape[0]
    r = mod3.shape[1]
    nt = t // tm
    c_t, s1_t, s2_t = tabs
    g_mix, w_in_p, g_q, w_q, g_kv = wts
    const = lambda i: (0, 0)
    tab_spec = pl.BlockSpec((tm, 128), lambda i: (i % tab_blocks, 0))
    row = lambda w: pl.BlockSpec((tm, w), lambda i: (i, 0))
    nck = tm // BQ
    out_shape = (
        jax.ShapeDtypeStruct((t, MLA_HEADS * QW), BF16),
        jax.ShapeDtypeStruct((t, QW), BF16),
        jax.ShapeDtypeStruct((t // BQ, 128, BQ), BF16),
        jax.ShapeDtypeStruct((t // (tab_blocks * tm), MLA_ROW, tab_blocks * tm), F32),
        jax.ShapeDtypeStruct((t, 512), BF16),
        jax.ShapeDtypeStruct((t * DIFF_HEADS, 128), F32),
        jax.ShapeDtypeStruct((t * DIFF_HEADS, 128), F32),
        jax.ShapeDtypeStruct((t, 512), BF16),
        jax.ShapeDtypeStruct((t // BQ, DIFF_HEADS, 128, BQ), BF16),
        jax.ShapeDtypeStruct((t, D_MODEL), BF16),
        jax.ShapeDtypeStruct((t, D_MODEL), BF16),
    )
    out_specs = (
        row(MLA_HEADS * QW), row(QW),
        pl.BlockSpec((nck, 128, BQ), lambda i: (i, 0, 0)),
        pl.BlockSpec((None, MLA_ROW, tm), lambda i: (i // tab_blocks, 0, i % tab_blocks)),
        row(512),
        pl.BlockSpec((tm * DIFF_HEADS, 128), lambda i: (i, 0)),
        pl.BlockSpec((tm * DIFF_HEADS, 128), lambda i: (i, 0)),
        row(512),
        pl.BlockSpec((nck, DIFF_HEADS, 128, BQ), lambda i: (i, 0, 0, 0)),
        row(D_MODEL), row(D_MODEL),
    )
    return pl.pallas_call(
        _in_kernel,
        grid=(nt,),
        in_specs=[row(D_MODEL), _mod_spec(r, tiles_per_group, 0), _mod_spec(r, tiles_per_group, 1),
                  pl.BlockSpec((1, D_MODEL), const),
                  pl.BlockSpec((D_MODEL, W_IN_COLS), const),
                  pl.BlockSpec((1, Q_LORA), const),
                  pl.BlockSpec((Q_LORA, MLA_HEADS * QW), const),
                  pl.BlockSpec((1, KV_LORA), const),
                  tab_spec, tab_spec, tab_spec],
        out_specs=out_specs,
        out_shape=out_shape,
        compiler_params=_cp("parallel"),
        name="mixer_in",
    )(x, mod3, mod3, g_mix, w_in_p, g_q, w_q, g_kv, c_t, s1_t, s2_t)


def _online_t(s, m_ref, l_ref):
    m_prev = m_ref[...]
    m_new = jnp.maximum(m_prev, jnp.max(s, axis=0, keepdims=True))
    alpha = jnp.exp2(m_prev - m_new)
    p = jnp.exp2(s - m_new[0:1, :])
    l_ref[...] = alpha * l_ref[...] + jnp.sum(p, axis=0, keepdims=True)
    m_ref[...] = m_new
    return p.astype(BF16), alpha[0:1, :]


def _attn_prompt_kernel(q_ref, kc_ref, ckvt_ref, dq_ref, dk_ref, dvt_ref, bias_ref,
                        lq1_ref, lk1_ref, lq2_ref, lk2_ref, gsub_ref,
                        oa_ref, ob_ref, m_a, l_a, acc_a, m_d, l_d, acc_d, qst, qbd):
    i = pl.program_id(1)
    wd = 2 * BQ
    m_a[...] = jnp.full(m_a.shape, NEG, F32)
    l_a[...] = jnp.zeros(l_a.shape, F32)
    acc_a[...] = jnp.zeros(acc_a.shape, F32)
    m_d[...] = jnp.full(m_d.shape, NEG, F32)
    l_d[...] = jnp.zeros(l_d.shape, F32)
    acc_d[...] = jnp.zeros(acc_d.shape, F32)
    for hd in range(MLA_HEADS):
        qst[hd * BQ:(hd + 1) * BQ, :] = q_ref[:, hd * QW:(hd + 1) * QW]
    lane = lax.broadcasted_iota(jnp.int32, (BQ, 128), 1)
    for hd in range(DIFF_HEADS):
        dqh = dq_ref[:, hd * 128:(hd + 1) * 128].astype(F32)
        qbd[hd, 0:BQ, :] = jnp.where(lane < DIFF_DH, dqh, 0.0).astype(BF16)
        qbd[hd, BQ:wd, :] = jnp.where(lane >= DIFF_DH, dqh, 0.0).astype(BF16)

    def step(j, masked):
        k0 = pl.multiple_of(j * BQ, BQ)
        if masked:
            krow = lax.broadcasted_iota(jnp.int32, (BQ, BQ), 0)
            qcol = lax.broadcasted_iota(jnp.int32, (BQ, BQ), 1)
            keep = krow <= qcol
        s = _dot_nt(kc_ref[pl.ds(k0, BQ), :], qst[...])
        if masked:
            s = jnp.where(jnp.concatenate([keep] * MLA_HEADS, axis=1), s, NEG)
        p, alpha = _online_t(s, m_a, l_a)
        acc_a[...] = alpha * acc_a[...] + _dot(ckvt_ref[j], p)

        parts, biases = [], []
        for hd in range(DIFF_HEADS):
            parts.append(_dot_nt(dk_ref[pl.ds(k0, BQ), hd * 128:(hd + 1) * 128], qbd[hd]))
            b = bias_ref[hd, i - j]
            biases += [b, b]
        s = jnp.concatenate(parts, axis=1) + jnp.concatenate(biases, axis=1)
        if masked:
            s = jnp.where(jnp.concatenate([keep] * (2 * DIFF_HEADS), axis=1), s, NEG)
        p, alpha = _online_t(s, m_d, l_d)
        for hd in range(DIFF_HEADS):
            cs = slice(hd * wd, (hd + 1) * wd)
            acc_d[:, cs] = alpha[:, cs] * acc_d[:, cs] + _dot(dvt_ref[j, hd], p[:, cs])

    def body(j, carry):
        step(j, False)
        return carry

    lax.fori_loop(0, i, body, 0)
    step(i, True)

    ot = acc_a[...] / l_a[0:1, :]
    for hd in range(MLA_HEADS):
        oa_ref[:, hd * 128:(hd + 1) * 128] = ot[:, hd * BQ:(hd + 1) * BQ].T.astype(BF16)
    lam = _lam(lq1_ref, lk1_ref, lq2_ref, lk2_ref)
    od = acc_d[...] / l_d[0:1, :]
    for hd in range(DIFF_HEADS):
        ot = od[:, hd * wd:hd * wd + BQ] - lam * od[:, hd * wd + BQ:(hd + 1) * wd]
        ms = jnp.mean(ot * ot, axis=0, keepdims=True)
        ot = ot * lax.rsqrt(ms + EPS) * gsub_ref[...] * (1.0 - LAM_INIT)
        ob_ref[:, hd * 128:(hd + 1) * 128] = ot.T.astype(BF16)


def _attn_prompt(q, kcat, ckvt, dq, dkb, dvt, bias_t, lam_vecs, gsub_col, batch, seq):
    t = q.shape[0]
    nq = seq // BQ
    qrow = lambda w: pl.BlockSpec((BQ, w), lambda b, i: (b * nq + i, 0))
    vec = pl.BlockSpec((1, DIFF_DH), lambda b, i: (0, 0))
    return pl.pallas_call(
        _attn_prompt_kernel,
        grid=(batch, nq),
        in_specs=[qrow(MLA_HEADS * QW),
                  pl.BlockSpec((seq, QW), lambda b, i: (b, 0)),
                  pl.BlockSpec((nq, 128, BQ), lambda b, i: (b, 0, 0)),
                  qrow(512),
                  pl.BlockSpec((seq, 512), lambda b, i: (b, 0)),
                  pl.BlockSpec((nq, DIFF_HEADS, 128, BQ), lambda b, i: (b, 0, 0, 0)),
                  pl.BlockSpec((DIFF_HEADS, nq, BQ, BQ), lambda b, i: (0, 0, 0, 0)),
                  vec, vec, vec, vec,
                  pl.BlockSpec((DIFF_DV, 1), lambda b, i: (0, 0))],
        out_specs=(qrow(MLA_HEADS * 128), qrow(512)),
        out_shape=(jax.ShapeDtypeStruct((t, MLA_HEADS * 128), BF16), jax.ShapeDtypeStruct((t, 512), BF16)),
        scratch_shapes=[pltpu.VMEM((8, MLA_HEADS * BQ), F32), pltpu.VMEM((8, MLA_HEADS * BQ), F32),
                        pltpu.VMEM((128, MLA_HEADS * BQ), F32),
                        pltpu.VMEM((8, DIFF_HEADS * 2 * BQ), F32), pltpu.VMEM((8, DIFF_HEADS * 2 * BQ), F32),
                        pltpu.VMEM((128, DIFF_HEADS * 2 * BQ), F32),
                        pltpu.VMEM((MLA_HEADS * BQ, QW), BF16),
                        pltpu.VMEM((DIFF_HEADS, 2 * BQ, 128), BF16)],
        compiler_params=_cp("parallel", "arbitrary"),
        name="attn_prompt",
    )(q, kcat, ckvt, dq, dkb, dvt, bias_t, *lam_vecs, gsub_col)


def _attn_sample_kernel(npg, pt_ref, q_ref, dq_ref, kcn_ref, dkn_ref, dvn_ref, biasn_ref, bias_ref,
                        lq1_ref, lk1_ref, lq2_ref, lk2_ref, gsub_ref, lat_hbm, k_hbm, v_hbm, oa_ref, ob_ref,
                        latp, kp, vp, sem, lat0, k0, v0, pgl, pgk, pgv, qs, qbd, m_a, l_a, acc_a, m_d, l_d, acc_d):
    b = pl.program_id(0)
    j = pl.program_id(1)
    nsteps = pl.num_programs(1)
    step = b * nsteps + j
    total = pl.num_programs(0) * nsteps
    slot = lax.rem(step, RING)
    nrow = MLA_HEADS * 8

    def page_copies(bb, jj, sl):
        cps = []
        for u in range(npg):
            page = pt_ref[bb, jj * npg + u]
            cps.append(pltpu.make_async_copy(lat_hbm.at[page], latp.at[sl, u], sem.at[sl]))
            cps.append(pltpu.make_async_copy(k_hbm.at[page], kp.at[sl, u], sem.at[sl]))
            cps.append(pltpu.make_async_copy(v_hbm.at[page], vp.at[sl, u], sem.at[sl]))
        return cps

    def start_all(cps):
        for n, cp in enumerate(cps):
            cp.start(priority=n % 2)

    def start_step(s):
        start_all(page_copies(lax.div(s, nsteps), lax.rem(s, nsteps), lax.rem(s, RING)))

    for ahead in range(RING - 1):
        @pl.when((step == 0) & (ahead < total))
        def _():
            start_step(jnp.int32(ahead))

    @pl.when(step + (RING - 1) < total)
    def _():
        start_step(step + (RING - 1))

    for cp in page_copies(b, j, slot):
        cp.wait()

    def update(lat_t, kmat, vmat, bias, mask):
        s = _dot(qs[...].astype(BF16), lat_t)
        if mask is not None:
            s = jnp.where(mask, s, NEG)
        m_prev = m_a[...]
        m_new = jnp.maximum(m_prev, jnp.max(s, axis=1, keepdims=True))
        alpha = jnp.exp2(m_prev - m_new)
        p = jnp.exp2(s - m_new[:, 0:1])
        l_a[...] = alpha * l_a[...] + jnp.sum(p, axis=1, keepdims=True)
        m_a[...] = m_new
        acc_a[...] = alpha * acc_a[...] + _dot_nt(p.astype(BF16), lat_t[0:KV_LORA, :])
        s = _dot_nt(qbd[...].astype(BF16), kmat) + bias
        if mask is not None:
            s = jnp.where(mask, s, NEG)
        m_prev = m_d[...]
        m_new = jnp.maximum(m_prev, jnp.max(s, axis=1, keepdims=True))
        alpha = jnp.exp2(m_prev - m_new)
        p = jnp.exp2(s - m_new[:, 0:1])
        l_d[...] = alpha * l_d[...] + jnp.sum(p, axis=1, keepdims=True)
        m_d[...] = m_new
        acc_d[...] = alpha[:, 0:1] * acc_d[...] + _dot(p.astype(BF16), vmat)

    def cast_pages():
        for u in range(npg):
            lat0[0:MLA_ROW, u * PAGE:(u + 1) * PAGE] = latp[slot, u].astype(BF16)
            for hd in range(DIFF_HEADS):
                rows = pl.ds(hd, PAGE, stride=DIFF_HEADS)
                k0[u * PAGE:(u + 1) * PAGE, hd * 128:(hd + 1) * 128] = kp[slot, u, rows, :].astype(BF16)
                v0[u * PAGE:(u + 1) * PAGE, hd * 128:(hd + 1) * 128] = vp[slot, u, rows, :].astype(BF16)

    @pl.when(j == 0)
    def _():
        m_a[...] = jnp.full(m_a.shape, NEG, F32)
        l_a[...] = jnp.zeros(l_a.shape, F32)
        acc_a[...] = jnp.zeros(acc_a.shape, F32)
        m_d[...] = jnp.full(m_d.shape, NEG, F32)
        l_d[...] = jnp.zeros(l_d.shape, F32)
        acc_d[...] = jnp.zeros(acc_d.shape, F32)
        lane = lax.broadcasted_iota(jnp.int32, (8, 512), 1)
        dq = dq_ref[...].astype(F32)
        for hd in range(MLA_HEADS):
            qs[hd * 8:(hd + 1) * 8, :] = q_ref[:, hd * QW:(hd + 1) * QW].astype(F32)
        for hd in range(DIFF_HEADS):
            for mp in range(2):
                lo = hd * 128 + mp * DIFF_DH
                r0 = (hd * 2 + mp) * 8
                qbd[r0:r0 + 8, :] = jnp.where((lane >= lo) & (lane < lo + DIFF_DH), dq, 0.0)
        pgl[...] = jnp.zeros(pgl.shape, F32)
        pgk[...] = jnp.zeros(pgk.shape, F32)
        pgv[...] = jnp.zeros(pgv.shape, F32)
        pgl[0:8, :] = kcn_ref[...].astype(F32)
        pgk[0:8, :] = dkn_ref[...]
        pgv[0:8, :] = dvn_ref[...]
        row = lax.broadcasted_iota(jnp.int32, (nrow, PAGE), 0)
        col = lax.broadcasted_iota(jnp.int32, (nrow, PAGE), 1)
        update(pgl[...].T.astype(BF16), pgk[...].astype(BF16), pgv[...].astype(BF16), biasn_ref[...],
               col <= (row % 8))
        lat0[MLA_ROW:QW, :] = jnp.zeros((QW - MLA_ROW, lat0.shape[1]), BF16)

    cast_pages()
    update(lat0[...], k0[...], v0[...], bias_ref[...], None)

    @pl.when(j == nsteps - 1)
    def _():
        oa = acc_a[...] / l_a[...]
        for hd in range(MLA_HEADS):
            oa_ref[:, hd * 128:(hd + 1) * 128] = oa[hd * 8:(hd + 1) * 8, :].astype(BF16)
        lam = _lam(lq1_ref, lk1_ref, lq2_ref, lk2_ref)
        od = acc_d[...] / l_d[:, 0:1]
        for hd in range(DIFF_HEADS):
            o1 = od[hd * 16:hd * 16 + 8, hd * 128:(hd + 1) * 128]
            o2 = od[hd * 16 + 8:hd * 16 + 16, hd * 128:(hd + 1) * 128]
            o = _rms(o1 - lam * o2, gsub_ref[...]) * (1.0 - LAM_INIT)
            ob_ref[:, hd * 128:(hd + 1) * 128] = o.astype(BF16)


def _attn_sample(q, dq, kcat, dk, dv, bias_new, bias_cache, lam_vecs, gsub_row, cache_mla_t, cache_k, cache_v,
                 page_table, npg):
    nb = page_table.shape[0]
    nsteps = page_table.shape[1] // npg
    nk = npg * PAGE
    q3 = q.reshape(nb, 8, MLA_HEADS * QW)
    dq3 = dq.reshape(nb, 8, 512)
    kc3 = kcat.reshape(nb, 8, QW)
    dk3 = dk.reshape(nb, 8, 512)
    dv3 = dv.reshape(nb, 8, 512)
    prow = PAGE * DIFF_HEADS
    ck = cache_k.reshape(-1, prow, 128)
    cv = cache_v.reshape(-1, prow, 128)
    seqblk = lambda w: pl.BlockSpec((None, 8, w), lambda b, j, pt: (b, 0, 0))
    vec = pl.BlockSpec((1, DIFF_DH), lambda b, j, pt: (0, 0))
    hbm = pl.BlockSpec(memory_space=pl.ANY)
    in_specs = [seqblk(MLA_HEADS * QW), seqblk(512), seqblk(QW), seqblk(512), seqblk(512),
                pl.BlockSpec((64, PAGE), lambda b, j, pt: (0, 0)),
                pl.BlockSpec((64, nk), lambda b, j, pt: (0, j)),
                vec, vec, vec, vec,
                pl.BlockSpec((1, DIFF_DV), lambda b, j, pt: (0, 0)),
                hbm, hbm, hbm]
    grid_spec = pltpu.PrefetchScalarGridSpec(
        num_scalar_prefetch=1,
        grid=(nb, nsteps),
        in_specs=in_specs,
        out_specs=(seqblk(MLA_HEADS * 128), seqblk(512)),
        scratch_shapes=[pltpu.VMEM((RING, npg, MLA_ROW, PAGE), F32), pltpu.VMEM((RING, npg, prow, 128), F32),
                        pltpu.VMEM((RING, npg, prow, 128), F32), pltpu.SemaphoreType.DMA((RING,)),
                        pltpu.VMEM((QW, nk), BF16), pltpu.VMEM((nk, 512), BF16), pltpu.VMEM((nk, 512), BF16),
                        pltpu.VMEM((PAGE, QW), F32), pltpu.VMEM((PAGE, 512), F32), pltpu.VMEM((PAGE, 512), F32),
                        pltpu.VMEM((64, QW), F32), pltpu.VMEM((64, 512), F32),
                        pltpu.VMEM((64, 128), F32), pltpu.VMEM((64, 128), F32), pltpu.VMEM((64, 128), F32),
                        pltpu.VMEM((64, 128), F32), pltpu.VMEM((64, 128), F32), pltpu.VMEM((64, 512), F32)],
    )
    oa, ob = pl.pallas_call(
        functools.partial(_attn_sample_kernel, npg),
        grid_spec=grid_spec,
        out_shape=(jax.ShapeDtypeStruct((nb, 8, MLA_HEADS * 128), BF16), jax.ShapeDtypeStruct((nb, 8, 512), BF16)),
        compiler_params=_cp("arbitrary", "arbitrary"),
        name="attn_sample",
    )(page_table, q3, dq3, kc3, dk3, dv3, bias_new, bias_cache, *lam_vecs, gsub_row, cache_mla_t, ck, cv)
    return oa.reshape(nb * 8, MLA_HEADS * 128), ob.reshape(nb * 8, 512)


def _first_max(v, idx, big):
    m = jnp.max(v, axis=0, keepdims=True)
    f = jnp.min(jnp.where(v == m, idx, big), axis=0, keepdims=True)
    return m, idx == f


def _mix_out_kernel(x_ref, oa_ref, ob_ref, ga_ref, gb_ref, gt1_ref, sh2_ref, sc2_ref, wova_ref, wb_ref, wo_ref, gffn_ref,
                    wrt_ref, rb_ref, x1_ref, h2_ref, gates_ref):
    tm = x_ref.shape[0]
    a = (ga_ref[...].astype(F32) * _dot(oa_ref[...], wova_ref[...])
         + gb_ref[...].astype(F32) * _dot(ob_ref[...], wb_ref[...]))
    x1 = x_ref[...] + gt1_ref[...] * _dot(a.astype(BF16), wo_ref[...])
    x1_ref[...] = x1
    h2 = _rms(x1, gffn_ref[...]) * (1.0 + sc2_ref[...]) + sh2_ref[...]
    h2_ref[...] = h2.astype(BF16)

    logits = lax.dot_general(wrt_ref[...], h2, (((1,), (1,)), ((), ())), preferred_element_type=F32,
                             precision=lax.Precision.HIGHEST)
    aff = jax.nn.sigmoid(logits)
    sel = aff + rb_ref[...]
    gsz = N_EXPERTS // N_GROUPS
    idx8 = lax.broadcasted_iota(jnp.int32, (gsz, tm), 0).astype(F32)
    scores = []
    for g in range(N_GROUPS):
        v = sel[g * gsz:(g + 1) * gsz, :]
        m1, hit = _first_max(v, idx8, float(gsz))
        m2 = jnp.max(jnp.where(hit, -jnp.inf, v), axis=0, keepdims=True)
        scores.append(m1 + m2)
    gs = jnp.concatenate(scores, axis=0)
    gidx = lax.broadcasted_iota(jnp.int32, (N_GROUPS, tm), 0).astype(F32)
    gsel = jnp.zeros((N_GROUPS, tm), F32)
    for _ in range(TOPK_GROUPS):
        _, hit = _first_max(gs, gidx, float(N_GROUPS))
        gsel = jnp.where(hit, 1.0, gsel)
        gs = jnp.where(hit, -jnp.inf, gs)
    cand = jnp.concatenate(
        [jnp.where(gsel[g:g + 1, :] > 0.0, sel[g * gsz:(g + 1) * gsz, :], -jnp.inf) for g in range(N_GROUPS)], axis=0)
    eidx = lax.broadcasted_iota(jnp.int32, (N_EXPERTS, tm), 0).astype(F32)
    w = jnp.zeros((N_EXPERTS, tm), F32)
    for _ in range(TOP_K):
        _, hit = _first_max(cand, eidx, float(N_EXPERTS))
        w = jnp.where(hit, aff, w)
        cand = jnp.where(hit, -jnp.inf, cand)
    gates_t = w / jnp.sum(w, axis=0, keepdims=True) * ROUTE_SCALE
    gates_ref[...] = jnp.concatenate([gates_t, jnp.zeros((128 - N_EXPERTS, tm), F32)], axis=0).T


def _mix_out(x, oa, ob, ga, gb, mod3, tiles_per_group, wts, tm):
    t = x.shape[0]
    r = mod3.shape[1]
    w_ova, w_b, w_o, g_ffn, w_rt, rb = wts
    const = lambda i: (0, 0)
    row = lambda w: pl.BlockSpec((tm, w), lambda i: (i, 0))
    return pl.pallas_call(
        _mix_out_kernel,
        grid=(t // tm,),
        in_specs=[row(D_MODEL), row(MLA_HEADS * 128), row(512), row(D_MODEL), row(D_MODEL),
                  _mod_spec(r, tiles_per_group, 2), _mod_spec(r, tiles_per_group, 3), _mod_spec(r, tiles_per_group, 4),
                  pl.BlockSpec((MLA_HEADS * 128, D_MODEL), const),
                  pl.BlockSpec((512, D_MODEL), const),
                  pl.BlockSpec((D_MODEL, D_MODEL), const),
                  pl.BlockSpec((1, D_MODEL), const),
                  pl.BlockSpec((N_EXPERTS, D_MODEL), const),
                  pl.BlockSpec((N_EXPERTS, 1), const)],
        out_specs=(row(D_MODEL), row(D_MODEL), row(128)),
        out_shape=(jax.ShapeDtypeStruct((t, D_MODEL), F32), jax.ShapeDtypeStruct((t, D_MODEL), BF16),
                   jax.ShapeDtypeStruct((t, 128), F32)),
        compiler_params=_cp("parallel"),
        name="mix_out",
    )(x, oa, ob, ga, gb, mod3, mod3, mod3, w_ova, w_b, w_o, g_ffn, w_rt, rb)


def _moe_kernel(h_ref, g_ref, wgu_ref, wd_ref, wsgu_ref, wsd_ref, x1_ref, gt2_ref, gfin_ref, o_ref, acc, hid_s):
    eg = pl.program_id(1)
    tm = h_ref.shape[0]
    h = h_ref[...]

    @pl.when(eg == 0)
    def _():
        sgu = _dot(h, wsgu_ref[...])
        hid = _silu(sgu[:, 0:D_SHARED]) * sgu[:, D_SHARED:2 * D_SHARED]
        acc[...] = _dot(hid.astype(BF16), wsd_ref[...])

    lane = lax.broadcasted_iota(jnp.int32, (tm, 128), 1)
    gates = g_ref[...]
    for k in range(MOE_EG):
        gu = _dot(h, wgu_ref[k])
        hid = _silu(gu[:, 0:D_EXPERT]) * gu[:, D_EXPERT:2 * D_EXPERT]
        g = jnp.sum(jnp.where(lane == eg * MOE_EG + k, gates, 0.0), axis=1, keepdims=True)
        hid_s[:, k * D_EXPERT:(k + 1) * D_EXPERT] = (hid * g).astype(BF16)
    acc[...] += _dot(hid_s[...], wd_ref[...].reshape(MOE_EG * D_EXPERT, D_MODEL))

    @pl.when(eg == pl.num_programs(1) - 1)
    def _():
        o_ref[...] = _rms(x1_ref[...] + gt2_ref[...] * acc[...], gfin_ref[...])


def _moe(h2, gates, x1, mod3, tiles_per_group, wts, tm):
    t = h2.shape[0]
    r = mod3.shape[1]
    w_gu, w_d, w_sgu, w_sd, g_fin = wts
    row = lambda w: pl.BlockSpec((tm, w), lambda i, e: (i, 0))
    const = lambda i, e: (0, 0)
    return pl.pallas_call(
        _moe_kernel,
        grid=(t // tm, N_EXPERTS // MOE_EG),
        in_specs=[row(D_MODEL), row(128),
                  pl.BlockSpec((MOE_EG, D_MODEL, 2 * D_EXPERT), lambda i, e: (e, 0, 0)),
                  pl.BlockSpec((MOE_EG, D_EXPERT, D_MODEL), lambda i, e: (e, 0, 0)),
                  pl.BlockSpec((D_MODEL, 2 * D_SHARED), const),
                  pl.BlockSpec((D_SHARED, D_MODEL), const),
                  row(D_MODEL), _mod_spec(r, tiles_per_group, 5, 2),
                  pl.BlockSpec((1, D_MODEL), const)],
        out_specs=row(D_MODEL),
        out_shape=jax.ShapeDtypeStruct((t, D_MODEL), F32),
        scratch_shapes=[pltpu.VMEM((tm, D_MODEL), F32), pltpu.VMEM((tm, MOE_EG * D_EXPERT), BF16)],
        compiler_params=_cp("parallel", "arbitrary"),
        name="moe",
    )(h2, gates, w_gu, w_d, w_sgu, w_sd, x1, mod3, g_fin)


def _t5_bucket(n):
    n = jnp.maximum(n, 0)
    max_exact = N_BUCKETS // 2
    nf = jnp.maximum(n, 1).astype(F32)
    large = max_exact + (jnp.log(nf / max_exact) / math.log(MAX_DISTANCE / max_exact)
                         * (N_BUCKETS - max_exact)).astype(jnp.int32)
    large = jnp.minimum(large, N_BUCKETS - 1)
    return jnp.where(n < max_exact, n, large)


def _t5_by_distance(rel_bias, nmax):
    hit = _t5_bucket(jnp.arange(nmax, dtype=jnp.int32))[:, None] == jnp.arange(N_BUCKETS, dtype=jnp.int32)[None, :]
    f = jnp.sum(jnp.where(hit[:, :, None], rel_bias[None].astype(F32), 0.0), axis=1)
    return f.T * LOG2E


def _t5_prompt_tiles(f, nq):
    h, c = f.shape[0], nq * BQ
    ln = c + BQ
    v = jnp.concatenate([f[:, :c], jnp.broadcast_to(f[:, 0:1], (h, BQ))], axis=1)
    m = jnp.tile(v, (1, BQ))[:, :BQ * (ln - 1)].reshape(h, BQ, ln - 1)[:, :, :c]
    return jnp.transpose(m.reshape(h, BQ, nq, BQ), (0, 2, 1, 3))


def _t5_sample_rows(f, past, dseq):
    rf = jnp.flip(f[:, :past + dseq], axis=1)
    per_t = jnp.stack([rf[:, dseq - 1 - t:dseq - 1 - t + past] for t in range(dseq)], axis=1)
    h = f.shape[0]
    return jnp.broadcast_to(per_t[:, None], (h, 2, dseq, past)).reshape(h * 2 * dseq, past)


def _rope_tables(pos):
    half = MLA_ROPE // 2
    inv = ROPE_THETA ** (-jnp.arange(half, dtype=F32) / half)
    ang = pos.astype(F32)[:, None] * inv[None, :]
    cos, sin = jnp.cos(ang), jnp.sin(ang)
    z = jnp.zeros((pos.shape[0], 128 - 2 * half), F32)
    zh = jnp.zeros_like(cos)
    return (jnp.concatenate([cos, cos, z], axis=1), jnp.concatenate([-sin, zh, z], axis=1),
            jnp.concatenate([zh, sin, z], axis=1))


def _tile(t, pref):
    while t % pref:
        pref //= 2
    return pref


def kernel(x_prompt, x_sample, c_prompt, c_sample, cache_mla, cache_k, cache_v, page_table, w_ada, b_ada, g_mix, w_in, g_q, w_uq, g_kv, w_uk, w_uv, lam_q1, lam_k1, lam_q2, lam_k2, g_subln, w_a, w_b, w_o, rel_bias, g_ffn, w_router, router_bias, w_exp_gu, w_exp_down, w_sh_gu, w_sh_down, g_final):
    batch, seq, _ = x_prompt.shape
    nb, dseq, _ = x_sample.shape
    past = page_table.shape[1] * PAGE
    l = 0
    tp, ts = batch * seq, nb * dseq

    w_in_l = w_in[l]
    w_in_p = jnp.concatenate([w_in_l[:, 0:416], jnp.zeros((D_MODEL, 96), F32), w_in_l[:, 416:]], axis=1).astype(BF16)
    uq = w_uq[l].reshape(Q_LORA, MLA_HEADS, MLA_NOPE + MLA_ROPE)
    w_ql = _bmm(jnp.transpose(uq[:, :, :MLA_NOPE], (1, 0, 2)), jnp.transpose(w_uk[l], (1, 2, 0)))
    w_qr = jnp.transpose(uq[:, :, MLA_NOPE:], (1, 0, 2))
    w_q = jnp.concatenate([w_ql, w_qr, jnp.zeros((MLA_HEADS, Q_LORA, QW - MLA_ROW), F32)], axis=2)
    w_q = w_q * (MLA_SCALE * LOG2E)
    w_q = jnp.transpose(w_q, (1, 0, 2)).reshape(Q_LORA, MLA_HEADS * QW).astype(BF16)
    w_ova = _bmm(jnp.transpose(w_uv[l], (1, 0, 2)), w_a[l].reshape(MLA_HEADS, 64, D_MODEL))
    w_ova = w_ova.reshape(MLA_HEADS * KV_LORA, D_MODEL).astype(BF16)
    in_wts = (g_mix[l].reshape(1, -1), w_in_p, g_q[l].reshape(1, -1), w_q, g_kv[l].reshape(1, -1))
    out_wts = (w_ova, w_b[l].astype(BF16), w_o[l].astype(BF16), g_ffn[l].reshape(1, -1),
               w_router[l].T, router_bias[l].reshape(-1, 1))
    moe_wts = (w_exp_gu[l].astype(BF16), w_exp_down[l].astype(BF16), w_sh_gu[l].astype(BF16),
               w_sh_down[l].astype(BF16), g_final.reshape(1, -1))
    lam_vecs = tuple(v[l].reshape(1, -1) for v in (lam_q1, lam_k1, lam_q2, lam_k2))

    mod = _ada(jnp.concatenate([c_prompt, c_sample], axis=0), w_ada[l], b_ada[l])
    mod_p = mod[:batch].reshape(batch, 1, 6 * D_MODEL)

    tm = _tile(seq, 512)
    xp = x_prompt.reshape(tp, D_MODEL)
    tabs_p = _rope_tables(jnp.arange(seq, dtype=jnp.int32))
    (q, kcat, ckvt, latt_p, dq, dk_p, dv_p, dkb, dvt, ga, gb) = _mixer_in(
        xp, mod_p, seq // tm, tabs_p, seq // tm, in_wts, tm)
    lat_p = jnp.transpose(latt_p, (0, 2, 1))
    nq = seq // BQ
    f_dist = _t5_by_distance(rel_bias, max(seq, past + dseq))
    bias_t = _t5_prompt_tiles(f_dist, nq)
    oa, ob = _attn_prompt(q, kcat, ckvt, dq, dkb, dvt, bias_t, lam_vecs, g_subln[l].reshape(-1, 1), batch, seq)
    x1, h2, gates = _mix_out(xp, oa, ob, ga, gb, mod_p, seq // tm, out_wts, tm)
    tmm = _tile(seq, 1024)
    y_p = _moe(h2, gates, x1, mod_p, seq // tmm, moe_wts, tmm)

    tms = _tile(ts, 256)
    xs = x_sample.reshape(ts, D_MODEL)
    mod_s = jnp.repeat(mod[batch:], dseq, axis=0).reshape(ts // tms, tms, 6 * D_MODEL)
    pos_s = past + (jnp.arange(ts, dtype=jnp.int32) % dseq)
    tabs_s = _rope_tables(pos_s)
    (q_s, kcat_s, _, latt_s, dq_s, dk_s, dv_s, _, _, ga_s, gb_s) = _mixer_in(
        xs, mod_s, 1, tabs_s, ts // tms, in_wts, tms)
    lat_s = jnp.transpose(latt_s[0], (1, 0))
    fz = jnp.concatenate([jnp.broadcast_to(f_dist[:, 0:1], (DIFF_HEADS, PAGE)), f_dist[:, :dseq]], axis=1)
    bias_new = _t5_sample_rows(fz, PAGE, dseq)
    bias_cache = _t5_sample_rows(f_dist, past, dseq)
    npg = next(n for n in (16, 8, 4, 2, 1) if page_table.shape[1] % n == 0)
    oa_s, ob_s = _attn_sample(q_s, dq_s, kcat_s, dk_s, dv_s, bias_new, bias_cache, lam_vecs,
                              g_subln[l].reshape(1, -1), jnp.transpose(cache_mla[l], (0, 2, 1)),
                              cache_k[l], cache_v[l], page_table, npg)
    x1_s, h2_s, gates_s = _mix_out(xs, oa_s, ob_s, ga_s, gb_s, mod_s, 1, out_wts, tms)
    tmm_s = _tile(ts, 512)
    y_s = _moe(h2_s, gates_s, x1_s, mod_s.reshape(ts // tmm_s, tmm_s, 6 * D_MODEL), 1, moe_wts, tmm_s)

    return (y_p.reshape(batch, seq, D_MODEL), y_s.reshape(nb, dseq, D_MODEL),
            lat_p.reshape(1, batch, seq, MLA_ROW), dk_p.reshape(1, batch, seq, DIFF_HEADS, 2 * DIFF_DH),
            dv_p.reshape(1, batch, seq, DIFF_HEADS, DIFF_DV),
            lat_s.reshape(1, nb, dseq, MLA_ROW), dk_s.reshape(1, nb, dseq, DIFF_HEADS, 2 * DIFF_DH),
            dv_s.reshape(1, nb, dseq, DIFF_HEADS, DIFF_DV))
```
